```python
import math
import jax, jax.numpy as jnp
from jax import lax
import numpy as np

D_MODEL = 1024
BATCH = 2
SEQ = 16384
DEPTH = 2
DEC_BATCH = 4
DEC_SEQ = 8192
PAST_LEN = 128

N_MEM = 256
EPS = 1e-6
HA = 4
DK_A = 128
DV_A = 256
CHUNK = 128
HB = 8
DH_B = 64
QBLOCK = 128
N_BUCKETS = 32
MAX_DIST = 128
HC = 4
DC = D_MODEL // HC
D_FF = 4 * D_MODEL
N_A = (DEPTH + 1) // 2
N_B = DEPTH // 2

kernel_name = 'hybrid_mlstm_diffattn_encoder'


def lambda_init_fn(layer):
    return 0.8 - 0.6 * math.exp(-0.3 * layer)


def rmsnorm(x, g):
    xf = x.astype(jnp.float32)
    y = xf * lax.rsqrt(jnp.mean(xf * xf, axis=-1, keepdims=True) + EPS)
    return (y * g.astype(jnp.float32)).astype(x.dtype)


def mlstm_scan(q, k, v, ig, lf):
    B, H, S, dk = q.shape
    dv = v.shape[-1]
    nc = S // CHUNK
    def chunks(a):
        return jnp.moveaxis(a.reshape(a.shape[:2] + (nc, CHUNK) + a.shape[3:]), 2, 0)
    tril = jnp.tril(jnp.ones((CHUNK, CHUNK), dtype=bool))

    def step(carry, inp):
        C, n, m = carry
        qc, kc, vc, igc, lfc = inp
        b = jnp.cumsum(lfc, axis=-1)
        dlog = jnp.where(tril, b[..., :, None] - b[..., None, :] + igc[..., None, :], -jnp.inf)
        inter = b + m[..., None]
        m_t = jnp.maximum(inter, jnp.max(dlog, axis=-1))
        s = jnp.einsum('bhtd,bhsd->bhts', qc, kc) * jnp.exp(dlog - m_t[..., None])
        iw = jnp.exp(inter - m_t)
        num = iw[..., None] * jnp.einsum('bhtd,bhde->bhte', qc, C) + jnp.einsum('bhts,bhse->bhte', s, vc)
        den = iw * jnp.einsum('bhtd,bhd->bht', qc, n) + jnp.sum(s, axis=-1)
        h = num / jnp.maximum(jnp.abs(den), jnp.exp(-m_t))[..., None]
        bl = b[..., -1]
        wlog = bl[..., None] - b + igc
        m_new = jnp.maximum(bl + m, jnp.max(wlog, axis=-1))
        w = jnp.exp(wlog - m_new[..., None])
        decay = jnp.exp(bl + m - m_new)
        C_new = decay[..., None, None] * C + jnp.einsum('bhs,bhsd,bhse->bhde', w, kc, vc)
        n_new = decay[..., None] * n + jnp.einsum('bhs,bhsd->bhd', w, kc)
        return (C_new, n_new, m_new), h

    init = (jnp.zeros((B, H, dk, dv), jnp.float32), jnp.zeros((B, H, dk), jnp.float32), jnp.zeros((B, H), jnp.float32))
    _, hs = lax.scan(step, init, (chunks(q), chunks(k), chunks(v), chunks(ig), chunks(lf)))
    return jnp.moveaxis(hs, 0, 2).reshape(B, H, S, dv)


def mlstm_mixer(xn, w_in, w_gate, b_gate, norm_g, w_out):
    B, S, _ = xn.shape
    q, k, v, o = jnp.split(xn @ w_in, [HA * DK_A, 2 * HA * DK_A, 2 * HA * DK_A + HA * DV_A], axis=-1)
    def heads(a, d):
        return a.reshape(B, S, HA, d).transpose(0, 2, 1, 3).astype(jnp.float32)
    q = heads(q, DK_A)
    k = heads(k, DK_A) * (DK_A ** -0.5)
    v = heads(v, DV_A)
    gates = (xn @ w_gate + b_gate).astype(jnp.float32).reshape(B, S, 4, HA).transpose(2, 0, 3, 1)
    ig_f, fg_f, ig_b, fg_b = gates[0], gates[1], gates[2], gates[3]
    h_f = mlstm_scan(q, k, v, ig_f, jax.nn.log_sigmoid(fg_f))
    fl = lambda a: jnp.flip(a, axis=2)
    h_b = fl(mlstm_scan(fl(q), fl(k), fl(v), fl(ig_b), fl(jax.nn.log_sigmoid(fg_b))))
    h = h_f + h_b
    hf = h * lax.rsqrt(jnp.mean(h * h, axis=-1, keepdims=True) + EPS)
    hf = hf.transpose(0, 2, 1, 3).reshape(B, S, HA * DV_A) * norm_g.astype(jnp.float32)
    out = hf * jax.nn.sigmoid(o.astype(jnp.float32))
    return out.astype(xn.dtype) @ w_out


def rel_bucket(rp):
    half = N_BUCKETS // 2
    max_exact = half // 2
    ret = (rp > 0).astype(jnp.int32) * half
    n = jnp.abs(rp)
    nf = jnp.maximum(n, 1).astype(jnp.float32)
    large = max_exact + (jnp.log(nf / max_exact) / math.log(MAX_DIST / max_exact) * (half - max_exact)).astype(jnp.int32)
    large = jnp.minimum(large, half - 1)
    return ret + jnp.where(n < max_exact, n, large)


def diff_attention(xn, w_qkv, lam_vecs, subln_g, w_out, rel_table, lambda_init):
    B, S, _ = xn.shape
    nblk = S // QBLOCK
    q, k, v = jnp.split(xn @ w_qkv, [2 * HB * DH_B, 4 * HB * DH_B], axis=-1)
    q = q.reshape(B, S, HB, 2, DH_B)
    k = k.reshape(B, S, HB, 2, DH_B)
    k1 = k[..., 0, :].transpose(0, 2, 1, 3)
    k2 = k[..., 1, :].transpose(0, 2, 1, 3)
    v = v.reshape(B, S, HB, 2 * DH_B).transpose(0, 2, 1, 3)
    def to_blocks(a):
        return a.reshape(B, nblk, QBLOCK, HB, DH_B).transpose(1, 0, 3, 2, 4)
    q1b = to_blocks(q[..., 0, :])
    q2b = to_blocks(q[..., 1, :])
    lv = lam_vecs.astype(jnp.float32)
    lam = jnp.exp(jnp.sum(lv[0] * lv[1])) - jnp.exp(jnp.sum(lv[2] * lv[3])) + lambda_init
    kpos = jnp.arange(S, dtype=jnp.int32)
    scale = DH_B ** -0.5
    table = rel_table.astype(jnp.float32)

    def block(args):
        q1, q2, q0 = args
        qpos = q0 + jnp.arange(QBLOCK, dtype=jnp.int32)
        bias = jnp.transpose(table[rel_bucket(kpos[None, :] - qpos[:, None])], (2, 0, 1))[None]
        p1 = jax.nn.softmax(jnp.einsum('bhqd,bhkd->bhqk', q1, k1).astype(jnp.float32) * scale + bias, axis=-1)
        p2 = jax.nn.softmax(jnp.einsum('bhqd,bhkd->bhqk', q2, k2).astype(jnp.float32) * scale + bias, axis=-1)
        a = (p1 - lam * p2).astype(v.dtype)
        return jnp.einsum('bhqk,bhke->bhqe', a, v)

    o = lax.map(block, (q1b, q2b, jnp.arange(nblk, dtype=jnp.int32) * QBLOCK))
    o = rmsnorm(o, subln_g) * (1.0 - lambda_init)
    o = o.transpose(1, 0, 3, 2, 4).reshape(B, S, HB * 2 * DH_B)
    return o @ w_out


def cross_attention(xn, memn, w_q, w_kv, w_out):
    B, S, _ = xn.shape
    M = memn.shape[1]
    q = (xn @ w_q).reshape(B, S, HC, DC)
    k, v = jnp.split(memn @ w_kv, 2, axis=-1)
    k = k.reshape(B, M, HC, DC)
    v = v.reshape(B, M, HC, DC)
    p = jax.nn.softmax(jnp.einsum('bshd,bmhd->bhsm', q, k).astype(jnp.float32) * (DC ** -0.5), axis=-1).astype(v.dtype)
    o = jnp.einsum('bhsm,bmhd->bshd', p, v).reshape(B, S, HC * DC)
    return o @ w_out


def squared_relu_mlp(xn, w1, w2):
    return jnp.square(jax.nn.relu(xn @ w1)) @ w2


def encoder(x, mem, g_mix, g_cross, g_mem, g_mlp, g_final, a_w_in, a_w_gate, a_b_gate, a_norm_g, a_w_out,
            b_w_qkv, b_lambda, b_subln_g, b_w_out, rel_bias, c_w_q, c_w_kv, c_w_out, f_w1, f_w2):
    for i in range(DEPTH):
        xn = rmsnorm(x, g_mix[i])
        j = i // 2
        if i % 2 == 0:
            x = x + mlstm_mixer(xn, a_w_in[j], a_w_gate[j], a_b_gate[j], a_norm_g[j], a_w_out[j])
        else:
            x = x + diff_attention(xn, b_w_qkv[j], b_lambda[j], b_subln_g[j], b_w_out[j], rel_bias, lambda_init_fn(i))
        x = x + cross_attention(rmsnorm(x, g_cross[i]), rmsnorm(mem, g_mem[i]), c_w_q[i], c_w_kv[i], c_w_out[i])
        x = x + squared_relu_mlp(rmsnorm(x, g_mlp[i]), f_w1[i], f_w2[i])
    return rmsnorm(x, g_final)


def setup_inputs(seed: int = 0) -> dict:
    key = jax.random.key(seed)
    ks = jax.random.split(key, 32)
    nrm = lambda k, shape, s: jax.random.normal(k, shape, jnp.float32) * s
    gain = lambda k, shape: 1.0 + nrm(k, shape, 0.02)
    gate_off = jnp.array([0.0, 3.0, 0.0, 3.0], jnp.float32)[None, :, None]
    gate_scale = jnp.array([0.1, 0.5, 0.1, 0.5], jnp.float32)[None, :, None]
    a_b_gate = (gate_off + gate_scale * jax.random.normal(ks[9], (N_A, 4, HA), jnp.float32)).reshape(N_A, 4 * HA)
    return {
        'x_prompt': nrm(ks[0], (BATCH, SEQ, D_MODEL), 1.0),
        'x_sample': nrm(ks[1], (DEC_BATCH, DEC_SEQ, D_MODEL), 1.0),
        'mem_prompt': nrm(ks[2], (BATCH, N_MEM, D_MODEL), 1.0),
        'mem_sample': nrm(ks[3], (DEC_BATCH, N_MEM, D_MODEL), 1.0),
        'g_mix': gain(ks[4], (DEPTH, D_MODEL)),
        'g_cross': gain(ks[5], (DEPTH, D_MODEL)),
        'g_mem': gain(ks[6], (DEPTH, D_MODEL)),
        'g_mlp': gain(ks[7], (DEPTH, D_MODEL)),
        'g_final': gain(ks[8], (D_MODEL,)),
        'a_w_in': nrm(ks[10], (N_A, D_MODEL, 2 * HA * DK_A + 2 * HA * DV_A), D_MODEL ** -0.5),
        'a_w_gate': nrm(ks[11], (N_A, D_MODEL, 4 * HA), D_MODEL ** -0.5),
        'a_b_gate': a_b_gate,
        'a_norm_g': gain(ks[12], (N_A, HA * DV_A)),
        'a_w_out': nrm(ks[13], (N_A, HA * DV_A, D_MODEL), (HA * DV_A) ** -0.5),
        'b_w_qkv': nrm(ks[14], (N_B, D_MODEL, 6 * HB * DH_B), D_MODEL ** -0.5),
        'b_lambda': nrm(ks[15], (N_B, 4, DH_B), 0.1),
        'b_subln_g': gain(ks[16], (N_B, 2 * DH_B)),
        'b_w_out': nrm(ks[17], (N_B, 2 * HB * DH_B, D_MODEL), (2 * HB * DH_B) ** -0.5),
        'rel_bias': nrm(ks[18], (N_BUCKETS, HB), 0.1),
        'c_w_q': nrm(ks[19], (DEPTH, D_MODEL, HC * DC), D_MODEL ** -0.5),
        'c_w_kv': nrm(ks[20], (DEPTH, D_MODEL, 2 * HC * DC), D_MODEL ** -0.5),
        'c_w_out': nrm(ks[21], (DEPTH, HC * DC, D_MODEL), (HC * DC) ** -0.5),
        'f_w1': nrm(ks[22], (DEPTH, D_MODEL, D_FF), D_MODEL ** -0.5),
        'f_w2': nrm(ks[23], (DEPTH, D_FF, D_MODEL), D_FF ** -0.5),
    }


def reference(x_prompt, x_sample, mem_prompt, mem_sample, g_mix, g_cross, g_mem, g_mlp, g_final,
              a_w_in, a_w_gate, a_b_gate, a_norm_g, a_w_out, b_w_qkv, b_lambda, b_subln_g, b_w_out,
              rel_bias, c_w_q, c_w_kv, c_w_out, f_w1, f_w2):
    y_prompt = encoder(x_prompt, mem_prompt, g_mix, g_cross, g_mem, g_mlp, g_final, a_w_in, a_w_gate, a_b_gate,
                       a_norm_g, a_w_out, b_w_qkv, b_lambda, b_subln_g, b_w_out, rel_bias, c_w_q, c_w_kv, c_w_out,
                       f_w1, f_w2)
    y_sample = encoder(x_sample, mem_sample, g_mix, g_cross, g_mem, g_mlp, g_final, a_w_in, a_w_gate, a_b_gate,
                       a_norm_g, a_w_out, b_w_qkv, b_lambda, b_subln_g, b_w_out, rel_bias, c_w_q, c_w_kv, c_w_out,
                       f_w1, f_w2)
    return (y_prompt, y_sample)
```

```python
import functools
import math

import jax
import jax.numpy as jnp
from jax import lax
from jax.experimental import pallas as pl
from jax.experimental.pallas import tpu as pltpu

EPS = 1e-6
HA = 4
DK_A = 128
DV_A = 256
CHUNK = 128
HB = 8
DH_B = 64
N_BUCKETS = 32
MAX_DIST = 128
HC = 4

F32 = jnp.float32
BF16 = jnp.bfloat16

VMEM_LIMIT_BYTES = 56 * 1024 * 1024
TOKEN_TILE = 512
NT_DIMS = (((1,), (1,)), ((), ()))


def _lambda_init(layer):
    return 0.8 - 0.6 * math.exp(-0.3 * layer)


def _params(*sem):
    return pltpu.CompilerParams(dimension_semantics=sem, vmem_limit_bytes=VMEM_LIMIT_BYTES)


def _resident(shape):
    zeros = (0,) * len(shape)
    return pl.BlockSpec(shape, lambda *_: zeros, pipeline_mode=pl.Buffered(1))


def _rms(x, g):
    return x * lax.rsqrt(jnp.mean(x * x, axis=-1, keepdims=True) + EPS) * g


def _dot(a, b):
    return jnp.dot(a, b, preferred_element_type=F32)


def _proj_a_kernel(x_ref, g_ref, w_ref, wgt_ref, bg_ref, qkv_ref, o_ref, gt_ref):
    xn = _rms(x_ref[...], g_ref[...]).astype(BF16)
    nqk = HA * DK_A
    for c in range(4):
        y = _dot(xn, w_ref[:, c * nqk:(c + 1) * nqk])
        if c == 1:
            y = y * (DK_A ** -0.5)
        qkv_ref[:, c * nqk:(c + 1) * nqk] = y.astype(BF16)
    for c in range(2):
        o_ref[:, c * nqk:(c + 1) * nqk] = _dot(xn, w_ref[:, (4 + c) * nqk:(5 + c) * nqk])
    gt = lax.dot_general(wgt_ref[...], xn, NT_DIMS, preferred_element_type=F32)
    gt_ref[...] = gt + bg_ref[...]


def _proj_a(x, g, w_in, w_gate_t, b_gate):
    B, S, D = x.shape
    ts = min(TOKEN_TILE, S)
    n_qkv = 2 * HA * DK_A + HA * DV_A
    n_o = HA * DV_A
    return pl.pallas_call(
        _proj_a_kernel,
        grid=(B, S // ts),
        in_specs=[
            pl.BlockSpec((None, ts, D), lambda b, i: (b, i, 0)),
            _resident((1, D)),
            _resident(w_in.shape),
            _resident(w_gate_t.shape),
            _resident(b_gate.shape),
        ],
        out_specs=[
            pl.BlockSpec((None, ts, n_qkv), lambda b, i: (b, i, 0)),
            pl.BlockSpec((None, ts, n_o), lambda b, i: (b, i, 0)),
            pl.BlockSpec((None, 4 * HA, ts), lambda b, i: (b, 0, i)),
        ],
        out_shape=[
            jax.ShapeDtypeStruct((B, S, n_qkv), BF16),
            jax.ShapeDtypeStruct((B, S, n_o), F32),
            jax.ShapeDtypeStruct((B, 4 * HA, S), F32),
        ],
        compiler_params=_params("parallel", "parallel"),
        name="proj_a",
    )(x, g, w_in, w_gate_t, b_gate)


def _log_sigmoid(x):
    return jnp.minimum(x, 0.0) - jnp.log1p(jnp.exp(-jnp.abs(x)))


def _mlstm_kernel(qf_ref, kf_ref, vf_ref, qb_ref, kb_ref, vb_ref, gf_ref, gb_ref,
                  hf_ref, hb_ref, c_ref, n_ref, m_ref, t_ref):
    L = CHUNK

    @pl.when(pl.program_id(1) == 0)
    def _():
        c_ref[...] = jnp.zeros(c_ref.shape, F32)
        n_ref[...] = jnp.zeros(n_ref.shape, F32)
        m_ref[...] = jnp.zeros(m_ref.shape, F32)
        t_ref[...] = jnp.zeros(t_ref.shape, F32)

    gf = gf_ref[...]
    gb = gb_ref[...]
    row8 = lax.broadcasted_iota(jnp.int32, (2 * HA, L), 0)
    lane8 = lax.broadcasted_iota(jnp.int32, (2 * HA, L), 1)
    is_f = row8 < HA
    ig8 = jnp.where(is_f, gf, pltpu.roll(gb, HA, 0))
    fg8 = jnp.where(is_f, pltpu.roll(gf, HA, 0), gb)
    lf8 = _log_sigmoid(fg8)

    b8 = lf8
    d = 1
    while d < L:
        fwd = jnp.where(lane8 >= d, pltpu.roll(b8, d, 1), 0.0)
        bwd = jnp.where(lane8 < L - d, pltpu.roll(b8, L - d, 1), 0.0)
        b8 = b8 + jnp.where(is_f, fwd, bwd)
        d *= 2

    bl8 = jnp.where(is_f[:, 0:1], b8[:, L - 1:L], b8[:, 0:1])
    m_old8 = m_ref[:, 0:1]
    a8 = ig8 - b8
    wlog8 = bl8 + a8
    m_new8 = jnp.maximum(bl8 + m_old8, jnp.max(wlog8, axis=1, keepdims=True))
    w8 = jnp.exp(wlog8 - m_new8)
    decay8 = jnp.exp(bl8 + m_old8 - m_new8)

    t_ref[0:2 * HA, :] = b8
    t_ref[2 * HA:4 * HA, :] = w8
    tt = t_ref[...].T

    ti = lax.broadcasted_iota(jnp.int32, (L, L), 0)
    si = lax.broadcasted_iota(jnp.int32, (L, L), 1)

    for r in range(2 * HA):
        h = r % HA
        if r < HA:
            q_ref, k_ref, v_ref, out_ref, mask = qf_ref, kf_ref, vf_ref, hf_ref, si <= ti
        else:
            q_ref, k_ref, v_ref, out_ref, mask = qb_ref, kb_ref, vb_ref, hb_ref, si >= ti
        q = q_ref[:, h * DK_A:(h + 1) * DK_A]
        k = k_ref[:, h * DK_A:(h + 1) * DK_A]
        v = v_ref[:, h * DV_A:(h + 1) * DV_A]
        b_col = tt[:, r:r + 1]
        w_col = tt[:, 2 * HA + r:2 * HA + r + 1]
        m_old = m_old8[r:r + 1, :]

        dlog = jnp.where(mask, b_col + a8[r:r + 1, :], -jnp.inf)
        inter = b_col + m_old
        m_t = jnp.maximum(inter, jnp.max(dlog, axis=1, keepdims=True))
        s = lax.dot_general(q, k, NT_DIMS, preferred_element_type=F32) * jnp.exp(dlog - m_t)
        iw = jnp.exp(inter - m_t)
        c_old = c_ref[r]
        n_old = n_ref[r:r + 1, :]
        num = iw * _dot(q, c_old.astype(BF16)) + _dot(s.astype(BF16), v)
        den = (iw * jnp.sum(q.astype(F32) * n_old, axis=1, keepdims=True)
               + jnp.sum(s, axis=1, keepdims=True))
        inv = 1.0 / jnp.maximum(jnp.abs(den), jnp.exp(-m_t))
        out_ref[:, h * DV_A:(h + 1) * DV_A] = num * inv

        wk = w_col * k.astype(F32)
        dec = decay8[r:r + 1, :]
        n_ref[r:r + 1, :] = dec * n_old + jnp.sum(wk, axis=0, keepdims=True)
        c_ref[r] = dec * c_old + _dot(wk.T.astype(BF16), v)

    m_ref[...] = jnp.broadcast_to(m_new8, m_ref.shape)


def _mlstm(qkv, gt):
    B, S, _ = qkv.shape
    L = CHUNK
    nc = S // L
    nqk = HA * DK_A
    nv = HA * DV_A

    def fwd(col):
        return lambda b, c: (b, c, col)

    def bwd(col):
        return lambda b, c: (b, nc - 1 - c, col)

    return pl.pallas_call(
        _mlstm_kernel,
        grid=(B, nc),
        in_specs=[
            pl.BlockSpec((None, L, nqk), fwd(0)),
            pl.BlockSpec((None, L, nqk), fwd(1)),
            pl.BlockSpec((None, L, nv), fwd(1)),
            pl.BlockSpec((None, L, nqk), bwd(0)),
            pl.BlockSpec((None, L, nqk), bwd(1)),
            pl.BlockSpec((None, L, nv), bwd(1)),
            pl.BlockSpec((None, 2 * HA, L), lambda b, c: (b, 0, c)),
            pl.BlockSpec((None, 2 * HA, L), lambda b, c: (b, 1, nc - 1 - c)),
        ],
        out_specs=[
            pl.BlockSpec((None, L, nv), fwd(0)),
            pl.BlockSpec((None, L, nv), bwd(0)),
        ],
        out_shape=[jax.ShapeDtypeStruct((B, S, nv), F32)] * 2,
        scratch_shapes=[
            pltpu.VMEM((2 * HA, DK_A, DV_A), F32),
            pltpu.VMEM((2 * HA, DK_A), F32),
            pltpu.VMEM((2 * HA, 128), F32),
            pltpu.VMEM((128, L), F32),
        ],
        compiler_params=_params("parallel", "arbitrary"),
        name="mlstm",
    )(qkv, qkv, qkv, qkv, qkv, qkv, gt, gt)


def _post_a_kernel(hf_ref, hb_ref, o_ref, x_ref, ng_ref, w_ref, out_ref):
    parts = []
    for h in range(HA):
        sl = slice(h * DV_A, (h + 1) * DV_A)
        hh = hf_ref[:, sl] + hb_ref[:, sl]
        hn = hh * lax.rsqrt(jnp.mean(hh * hh, axis=-1, keepdims=True) + EPS)
        hn = hn * ng_ref[:, sl] * jax.nn.sigmoid(o_ref[:, sl])
        parts.append(hn.astype(BF16))
    out_ref[...] = x_ref[...] + _dot(jnp.concatenate(parts, axis=1), w_ref[...])


def _post_a(hf, hb, o, x, norm_g, w_out):
    B, S, D = x.shape
    nv = hf.shape[-1]
    ts = min(TOKEN_TILE, S)
    tile = lambda n: pl.BlockSpec((None, ts, n), lambda b, i: (b, i, 0))
    return pl.pallas_call(
        _post_a_kernel,
        grid=(B, S // ts),
        in_specs=[tile(nv), tile(nv), tile(nv), tile(D), _resident((1, nv)), _resident(w_out.shape)],
        out_specs=tile(D),
        out_shape=jax.ShapeDtypeStruct((B, S, D), F32),
        compiler_params=_params("parallel", "parallel"),
        name="post_a",
    )(hf, hb, o, x, norm_g, w_out)


def _proj_b_kernel(x_ref, g_ref, wqt_ref, wk_ref, wvt_ref, qt_ref, k_ref, vt_ref):
    xn = _rms(x_ref[...], g_ref[...]).astype(BF16)
    qt = lax.dot_general(wqt_ref[...], xn, NT_DIMS, preferred_element_type=F32)
    qt_ref[...] = (qt * (DH_B ** -0.5)).astype(BF16)
    vt = lax.dot_general(wvt_ref[...], xn, NT_DIMS, preferred_element_type=F32)
    vt_ref[...] = vt.astype(BF16)
    k_ref[...] = _dot(xn, wk_ref[...]).astype(BF16)


def _proj_b(x, g, w_q_t, w_k, w_v_t, t):
    B, S, D = x.shape
    nb = S // t
    n = w_k.shape[1]
    return pl.pallas_call(
        _proj_b_kernel,
        grid=(B, nb),
        in_specs=[
            pl.BlockSpec((None, t, D), lambda b, i: (b, i, 0)),
            _resident((1, D)),
            _resident(w_q_t.shape),
            _resident(w_k.shape),
            _resident(w_v_t.shape),
        ],
        out_specs=[
            pl.BlockSpec((None, None, n, t), lambda b, i: (b, i, 0, 0)),
            pl.BlockSpec((None, t, n), lambda b, i: (b, i, 0)),
            pl.BlockSpec((None, None, n, t), lambda b, i: (b, i, 0, 0)),
        ],
        out_shape=[
            jax.ShapeDtypeStruct((B, nb, n, t), BF16),
            jax.ShapeDtypeStruct((B, S, n), BF16),
            jax.ShapeDtypeStruct((B, nb, n, t), BF16),
        ],
        compiler_params=_params("parallel", "parallel"),
        name="proj_b",
    )(x, g, w_q_t, w_k, w_v_t)


def _bias_kernel(tab_ref, out_ref, *, t):
    h = pl.program_id(0)
    off = (pl.program_id(1) - 1) * t
    ki = lax.broadcasted_iota(jnp.int32, (t, t), 0)
    qi = lax.broadcasted_iota(jnp.int32, (t, t), 1)
    rp = off + ki - qi
    half = N_BUCKETS // 2
    max_exact = half // 2
    ret = (rp > 0).astype(jnp.int32) * half
    n = jnp.abs(rp)
    nf = jnp.maximum(n, 1).astype(F32)
    large = max_exact + (jnp.log(nf / max_exact) / math.log(MAX_DIST / max_exact)
                         * (half - max_exact)).astype(jnp.int32)
    large = jnp.minimum(large, half - 1)
    bucket = ret + jnp.where(n < max_exact, n, large)
    acc = jnp.zeros((t, t), F32)
    for b in range(N_BUCKETS):
        acc = jnp.where(bucket == b, tab_ref[b, h], acc)
    out_ref[...] = acc


def _bias_tiles(rel_bias, t):
    return pl.pallas_call(
        functools.partial(_bias_kernel, t=t),
        grid=(HB, 3),
        in_specs=[pl.BlockSpec(memory_space=pltpu.SMEM)],
        out_specs=pl.BlockSpec((None, None, t, t), lambda h, j: (h, j, 0, 0)),
        out_shape=jax.ShapeDtypeStruct((HB, 3, t, t), F32),
        compiler_params=_params("parallel", "parallel"),
        name="bias_tiles",
    )(rel_bias)


def _attn_kernel(qt_ref, k_ref, vt_ref, bias_ref, cfar_ref, lam_ref, g_ref, o_ref,
                 q1_ref, q2_ref, m_ref, l_ref, acc_ref, *, t, nk, lambda_init):
    h = pl.program_id(1)
    qi = pl.program_id(2)

    qt = qt_ref[...]
    rows = lax.broadcasted_iota(jnp.int32, qt.shape, 0)
    q1_ref[...] = jnp.where(rows < DH_B, qt, jnp.zeros_like(qt))
    q2_ref[...] = jnp.where(rows >= DH_B, qt, jnp.zeros_like(qt))
    m_ref[...] = jnp.full(m_ref.shape, -jnp.inf, F32)
    l_ref[...] = jnp.zeros(l_ref.shape, F32)
    acc_ref[...] = jnp.zeros(acc_ref.shape, F32)

    def step(kj, bias_idx, c):
        ks = pl.multiple_of(kj * t, t)
        kb = k_ref[pl.ds(ks, t), :]
        vt = vt_ref[kj]
        for j, q_ref in enumerate((q1_ref, q2_ref)):
            s = _dot(kb, q_ref[...])
            if bias_idx is not None:
                s = s + bias_ref[bias_idx]
            m_old = m_ref[j:j + 1, :]
            mb = jnp.max(s, axis=0, keepdims=True)
            if bias_idx is None:
                mb = mb + c
            m_new = jnp.maximum(m_old, mb)
            shift = m_new if bias_idx is not None else m_new - c
            p = jnp.exp(s - shift)
            alpha = jnp.exp(m_old - m_new)
            l_ref[j:j + 1, :] = alpha * l_ref[j:j + 1, :] + jnp.sum(p, axis=0, keepdims=True)
            acc_ref[j] = alpha * acc_ref[j] + _dot(vt, p.astype(BF16))
            m_ref[j:j + 1, :] = m_new

    def far(c):
        def body(kj, carry):
            step(kj, None, c)
            return carry
        return body

    lax.fori_loop(0, jnp.maximum(qi - 1, 0), far(cfar_ref[h, 0]), 0)

    @pl.when(qi >= 1)
    def _():
        step(qi - 1, 0, None)

    step(qi, 1, None)

    @pl.when(qi + 1 < nk)
    def _():
        step(qi + 1, 2, None)

    lax.fori_loop(qi + 2, nk, far(cfar_ref[h, 1]), 0)

    lv = lam_ref[...]
    lam = (jnp.exp(jnp.sum(lv[0:1] * lv[1:2], axis=1, keepdims=True))
           - jnp.exp(jnp.sum(lv[2:3] * lv[3:4], axis=1, keepdims=True)) + lambda_init)
    o = (acc_ref[0] * (1.0 / l_ref[0:1, :])
         - lam * (acc_ref[1] * (1.0 / l_ref[1:2, :])))
    on = o * lax.rsqrt(jnp.mean(o * o, axis=0, keepdims=True) + EPS) * g_ref[...]
    on = on * (1.0 - lambda_init)
    o_ref[...] = on.T.astype(BF16)


def _attn(qt, k, vt, bias, cfar, lam_vecs, subln_g, lambda_init):
    B, nb, n, t = qt.shape
    S = k.shape[1]
    dh2 = 2 * DH_B
    return pl.pallas_call(
        functools.partial(_attn_kernel, t=t, nk=nb, lambda_init=lambda_init),
        grid=(B, HB, nb),
        in_specs=[
            pl.BlockSpec((None, None, dh2, t), lambda b, h, i: (b, i, h, 0)),
            pl.BlockSpec((None, S, dh2), lambda b, h, i: (b, 0, h)),
            pl.BlockSpec((None, nb, dh2, t), lambda b, h, i: (b, 0, h, 0)),
            pl.BlockSpec((None, 3, t, t), lambda b, h, i: (h, 0, 0, 0)),
            pl.BlockSpec(memory_space=pltpu.SMEM),
            _resident(lam_vecs.shape),
            _resident(subln_g.shape),
        ],
        out_specs=pl.BlockSpec((None, t, dh2), lambda b, h, i: (b, i, h)),
        out_shape=jax.ShapeDtypeStruct((B, S, n), BF16),
        scratch_shapes=[
            pltpu.VMEM((dh2, t), BF16),
            pltpu.VMEM((dh2, t), BF16),
            pltpu.VMEM((2, t), F32),
            pltpu.VMEM((2, t), F32),
            pltpu.VMEM((2, dh2, t), F32),
        ],
        compiler_params=_params("parallel", "parallel", "parallel"),
        name="diff_attn",
    )(qt, k, vt, bias, cfar, lam_vecs, subln_g)


def _out_b_kernel(o_ref, x_ref, w_ref, out_ref):
    out_ref[...] = x_ref[...] + _dot(o_ref[...], w_ref[...])


def _out_b(o, x, w_out):
    B, S, D = x.shape
    ts = min(TOKEN_TILE, S)
    tile = lambda n: pl.BlockSpec((None, ts, n), lambda b, i: (b, i, 0))
    return pl.pallas_call(
        _out_b_kernel,
        grid=(B, S // ts),
        in_specs=[tile(o.shape[-1]), tile(D), _resident(w_out.shape)],
        out_specs=tile(D),
        out_shape=jax.ShapeDtypeStruct((B, S, D), F32),
        compiler_params=_params("parallel", "parallel"),
        name="out_b",
    )(o, x, w_out)


def _mem_kv_kernel(mem_ref, g_ref, w_ref, kv_ref):
    mn = _rms(mem_ref[...], g_ref[...]).astype(BF16)
    kv_ref[...] = _dot(mn, w_ref[...]).astype(BF16)


def _mem_kv(mem, g, w_kv):
    B, M, D = mem.shape
    n = w_kv.shape[1]
    return pl.pallas_call(
        _mem_kv_kernel,
        grid=(B,),
        in_specs=[pl.BlockSpec((None, M, D), lambda b: (b, 0, 0)), _resident((1, D)),
                  _resident(w_kv.shape)],
        out_specs=pl.BlockSpec((None, M, n), lambda b: (b, 0, 0)),
        out_shape=jax.ShapeDtypeStruct((B, M, n), BF16),
        compiler_params=_params("parallel"),
        name="mem_kv",
    )(mem, g, w_kv)


def _cross_kernel(x_ref, g_ref, wq_ref, kv_ref, wo_ref, out_ref):
    x = x_ref[...]
    D = x.shape[-1]
    dc = D // HC
    xn = _rms(x, g_ref[...]).astype(BF16)
    q = _dot(xn, wq_ref[...])
    parts = []
    for h in range(HC):
        qh = (q[:, h * dc:(h + 1) * dc] * (dc ** -0.5)).astype(BF16)
        kh = kv_ref[:, h * dc:(h + 1) * dc]
        vh = kv_ref[:, D + h * dc:D + (h + 1) * dc]
        s = lax.dot_general(qh, kh, NT_DIMS, preferred_element_type=F32)
        p = jnp.exp(s - jnp.max(s, axis=-1, keepdims=True))
        inv = 1.0 / jnp.sum(p, axis=-1, keepdims=True)
        parts.append((_dot(p.astype(BF16), vh) * inv).astype(BF16))
    out_ref[...] = x + _dot(jnp.concatenate(parts, axis=1), wo_ref[...])


def _cross(x, g, w_q, kv, w_out):
    B, S, D = x.shape
    M = kv.shape[1]
    ts = min(TOKEN_TILE, S)
    tile = pl.BlockSpec((None, ts, D), lambda b, i: (b, i, 0))
    return pl.pallas_call(
        _cross_kernel,
        grid=(B, S // ts),
        in_specs=[tile, _resident((1, D)), _resident(w_q.shape),
                  pl.BlockSpec((None, M, 2 * D), lambda b, i: (b, 0, 0)),
                  _resident(w_out.shape)],
        out_specs=tile,
        out_shape=jax.ShapeDtypeStruct((B, S, D), F32),
        compiler_params=_params("parallel", "parallel"),
        name="cross_attn",
    )(x, g, w_q, kv, w_out)


def _mlp_kernel(x_ref, g_ref, w1_ref, w2_ref, gf_ref, out_ref, *, final_norm):
    x = x_ref[...]
    D = x.shape[-1]
    xn = _rms(x, g_ref[...]).astype(BF16)
    acc = x
    for c in range(w1_ref.shape[1] // D):
        hcol = _dot(xn, w1_ref[:, c * D:(c + 1) * D])
        hcol = jnp.square(jnp.maximum(hcol, 0.0)).astype(BF16)
        acc = acc + _dot(hcol, w2_ref[c * D:(c + 1) * D, :])
    out_ref[...] = _rms(acc, gf_ref[...]) if final_norm else acc


def _mlp(x, g, w1, w2, g_final, final_norm):
    B, S, D = x.shape
    ts = min(TOKEN_TILE, S)
    tile = pl.BlockSpec((None, ts, D), lambda b, i: (b, i, 0))
    return pl.pallas_call(
        functools.partial(_mlp_kernel, final_norm=final_norm),
        grid=(B, S // ts),
        in_specs=[tile, _resident((1, D)), _resident(w1.shape), _resident(w2.shape),
                  _resident((1, D))],
        out_specs=tile,
        out_shape=jax.ShapeDtypeStruct((B, S, D), F32),
        compiler_params=_params("parallel", "parallel"),
        name="mlp",
    )(x, g, w1, w2, g_final)


def _prepare(g_mix, g_cross, g_mem, g_mlp, g_final, a_w_in, a_w_gate, a_b_gate, a_norm_g,
             a_w_out, b_w_qkv, b_lambda, b_subln_g, b_w_out, rel_bias, c_w_q, c_w_kv,
             c_w_out, f_w1, f_w2):
    row = lambda v: v.reshape(1, -1).astype(F32)
    nq = 2 * HB * DH_B
    wqkv = b_w_qkv[0]
    return dict(
        g_mix=[row(g) for g in g_mix], g_cross=[row(g) for g in g_cross],
        g_mem=[row(g) for g in g_mem], g_mlp=[row(g) for g in g_mlp], g_final=row(g_final),
        a_w_in=a_w_in[0].astype(BF16), a_w_gate_t=a_w_gate[0].T.astype(BF16),
        a_b_gate=a_b_gate[0].reshape(-1, 1).astype(F32), a_norm_g=row(a_norm_g[0]),
        a_w_out=a_w_out[0].astype(BF16),
        b_w_q_t=wqkv[:, :nq].T.astype(BF16), b_w_k=wqkv[:, nq:2 * nq].astype(BF16),
        b_w_v_t=wqkv[:, 2 * nq:].T.astype(BF16),
        b_lambda=b_lambda[0].astype(F32), b_subln_g=b_subln_g[0].reshape(-1, 1).astype(F32),
        b_w_out=b_w_out[0].astype(BF16),
        rel_bias=rel_bias.astype(F32),
        cfar=jnp.stack([rel_bias[N_BUCKETS // 2 - 1], rel_bias[N_BUCKETS - 1]], axis=1).astype(F32),
        c_w_q=[w.astype(BF16) for w in c_w_q], c_w_kv=[w.astype(BF16) for w in c_w_kv],
        c_w_out=[w.astype(BF16) for w in c_w_out],
        f_w1=[w.astype(BF16) for w in f_w1], f_w2=[w.astype(BF16) for w in f_w2],
    )


def _encoder(x, mem, p, bias, t):
    depth = len(p["g_mix"])
    for i in range(depth):
        if i % 2 == 0:
            qkv, o, gt = _proj_a(x, p["g_mix"][i], p["a_w_in"], p["a_w_gate_t"], p["a_b_gate"])
            hf, hb = _mlstm(qkv, gt)
            x = _post_a(hf, hb, o, x, p["a_norm_g"], p["a_w_out"])
        else:
            qt, k, vt = _proj_b(x, p["g_mix"][i], p["b_w_q_t"], p["b_w_k"], p["b_w_v_t"], t)
            o = _attn(qt, k, vt, bias, p["cfar"], p["b_lambda"], p["b_subln_g"], _lambda_init(i))
            x = _out_b(o, x, p["b_w_out"])
        kv = _mem_kv(mem, p["g_mem"][i], p["c_w_kv"][i])
        x = _cross(x, p["g_cross"][i], p["c_w_q"][i], kv, p["c_w_out"][i])
        x = _mlp(x, p["g_mlp"][i], p["f_w1"][i], p["f_w2"][i], p["g_final"], i == depth - 1)
    return x


def kernel(x_prompt, x_sample, mem_prompt, mem_sample, g_mix, g_cross, g_mem, g_mlp, g_final, a_w_in, a_w_gate, a_b_gate, a_norm_g, a_w_out, b_w_qkv, b_lambda, b_subln_g, b_w_out, rel_bias, c_w_q, c_w_kv, c_w_out, f_w1, f_w2):
    p = _prepare(g_mix, g_cross, g_mem, g_mlp, g_final, a_w_in, a_w_gate, a_b_gate, a_norm_g,
                 a_w_out, b_w_qkv, b_lambda, b_subln_g, b_w_out, rel_bias, c_w_q, c_w_kv,
                 c_w_out, f_w1, f_w2)
    t = min(TOKEN_TILE, x_prompt.shape[1], x_sample.shape[1])
    bias = _bias_tiles(p["rel_bias"], t)
    y_prompt = _encoder(x_prompt, mem_prompt, p, bias, t)
    y_sample = _encoder(x_sample, mem_sample, p, bias, t)
    return (y_prompt, y_sample)
```

```python
import functools
import math

import jax
import jax.numpy as jnp
from jax import lax
from jax.experimental import pallas as pl
from jax.experimental.pallas import tpu as pltpu

EPS = 1e-6
HA = 4
DK_A = 128
DV_A = 256
CHUNK = 128
HB = 8
DH_B = 64
N_BUCKETS = 32
MAX_DIST = 128
HC = 4

F32 = jnp.float32
BF16 = jnp.bfloat16

VMEM_LIMIT_BYTES = 56 * 1024 * 1024
TOKEN_TILE = 512
NT_DIMS = (((1,), (1,)), ((), ()))
LOG2E = math.log2(math.e)


def _lambda_init(layer):
    return 0.8 - 0.6 * math.exp(-0.3 * layer)


def _params(*sem):
    return pltpu.CompilerParams(dimension_semantics=sem, vmem_limit_bytes=VMEM_LIMIT_BYTES)


def _resident(shape):
    zeros = (0,) * len(shape)
    return pl.BlockSpec(shape, lambda *_: zeros, pipeline_mode=pl.Buffered(1))


def _rms(x, g):
    return x * lax.rsqrt(jnp.mean(x * x, axis=-1, keepdims=True) + EPS) * g


def _dot(a, b):
    return jnp.dot(a, b, preferred_element_type=F32)


def _proj_a_kernel(x_ref, g_ref, w_ref, wgt_ref, bg_ref, qkv_ref, o_ref, gt_ref):
    xn = _rms(x_ref[...], g_ref[...]).astype(BF16)
    nqk = HA * DK_A
    for c in range(4):
        y = _dot(xn, w_ref[:, c * nqk:(c + 1) * nqk])
        if c == 1:
            y = y * (DK_A ** -0.5)
        qkv_ref[:, c * nqk:(c + 1) * nqk] = y.astype(BF16)
    for c in range(2):
        o_ref[:, c * nqk:(c + 1) * nqk] = _dot(xn, w_ref[:, (4 + c) * nqk:(5 + c) * nqk])
    gt = lax.dot_general(wgt_ref[...], xn, NT_DIMS, preferred_element_type=F32)
    gt_ref[...] = gt + bg_ref[...]


def _proj_a(x, g, w_in, w_gate_t, b_gate):
    B, S, D = x.shape
    ts = min(TOKEN_TILE, S)
    n_qkv = 2 * HA * DK_A + HA * DV_A
    n_o = HA * DV_A
    return pl.pallas_call(
        _proj_a_kernel,
        grid=(B, S // ts),
        in_specs=[
            pl.BlockSpec((None, ts, D), lambda b, i: (b, i, 0)),
            _resident((1, D)),
            _resident(w_in.shape),
            _resident(w_gate_t.shape),
            _resident(b_gate.shape),
        ],
        out_specs=[
            pl.BlockSpec((None, ts, n_qkv), lambda b, i: (b, i, 0)),
            pl.BlockSpec((None, ts, n_o), lambda b, i: (b, i, 0)),
            pl.BlockSpec((None, 4 * HA, ts), lambda b, i: (b, 0, i)),
        ],
        out_shape=[
            jax.ShapeDtypeStruct((B, S, n_qkv), BF16),
            jax.ShapeDtypeStruct((B, S, n_o), F32),
            jax.ShapeDtypeStruct((B, 4 * HA, S), F32),
        ],
        compiler_params=_params("parallel", "parallel"),
        name="proj_a",
    )(x, g, w_in, w_gate_t, b_gate)


def _log_sigmoid(x):
    return jnp.minimum(x, 0.0) - jnp.log1p(jnp.exp(-jnp.abs(x)))


def _mlstm_kernel(qf_ref, kf_ref, vf_ref, qb_ref, kb_ref, vb_ref, gf_ref, gb_ref,
                  hf_ref, hb_ref, c_ref, n_ref, m_ref, t_ref):
    L = CHUNK

    @pl.when(pl.program_id(1) == 0)
    def _():
        c_ref[...] = jnp.zeros(c_ref.shape, F32)
        n_ref[...] = jnp.zeros(n_ref.shape, F32)
        m_ref[...] = jnp.zeros(m_ref.shape, F32)
        t_ref[...] = jnp.zeros(t_ref.shape, F32)

    gf = gf_ref[...]
    gb = gb_ref[...]
    row8 = lax.broadcasted_iota(jnp.int32, (2 * HA, L), 0)
    lane8 = lax.broadcasted_iota(jnp.int32, (2 * HA, L), 1)
    is_f = row8 < HA
    ig8 = jnp.where(is_f, gf, pltpu.roll(gb, HA, 0))
    fg8 = jnp.where(is_f, pltpu.roll(gf, HA, 0), gb)
    lf8 = _log_sigmoid(fg8)

    b8 = lf8
    d = 1
    while d < L:
        fwd = jnp.where(lane8 >= d, pltpu.roll(b8, d, 1), 0.0)
        bwd = jnp.where(lane8 < L - d, pltpu.roll(b8, L - d, 1), 0.0)
        b8 = b8 + jnp.where(is_f, fwd, bwd)
        d *= 2

    bl8 = jnp.where(is_f[:, 0:1], b8[:, L - 1:L], b8[:, 0:1])
    m_old8 = m_ref[:, 0:1]
    a8 = ig8 - b8
    wlog8 = bl8 + a8
    m_new8 = jnp.maximum(bl8 + m_old8, jnp.max(wlog8, axis=1, keepdims=True))
    w8 = jnp.exp(wlog8 - m_new8)
    decay8 = jnp.exp(bl8 + m_old8 - m_new8)

    t_ref[0:2 * HA, :] = b8
    t_ref[2 * HA:4 * HA, :] = w8
    tt = t_ref[...].T

    ti = lax.broadcasted_iota(jnp.int32, (L, L), 0)
    si = lax.broadcasted_iota(jnp.int32, (L, L), 1)

    for r in range(2 * HA):
        h = r % HA
        if r < HA:
            q_ref, k_ref, v_ref, out_ref, mask = qf_ref, kf_ref, vf_ref, hf_ref, si <= ti
        else:
            q_ref, k_ref, v_ref, out_ref, mask = qb_ref, kb_ref, vb_ref, hb_ref, si >= ti
        q = q_ref[:, h * DK_A:(h + 1) * DK_A]
        k = k_ref[:, h * DK_A:(h + 1) * DK_A]
        v = v_ref[:, h * DV_A:(h + 1) * DV_A]
        b_col = tt[:, r:r + 1]
        w_col = tt[:, 2 * HA + r:2 * HA + r + 1]
        m_old = m_old8[r:r + 1, :]

        dlog = jnp.where(mask, b_col + a8[r:r + 1, :], -jnp.inf)
        inter = b_col + m_old
        m_t = jnp.maximum(inter, jnp.max(dlog, axis=1, keepdims=True))
        s = lax.dot_general(q, k, NT_DIMS, preferred_element_type=F32) * jnp.exp(dlog - m_t)
        iw = jnp.exp(inter - m_t)
        c_old = c_ref[r]
        n_old = n_ref[r:r + 1, :]
        num = iw * _dot(q, c_old.astype(BF16)) + _dot(s.astype(BF16), v)
        den = (iw * jnp.sum(q.astype(F32) * n_old, axis=1, keepdims=True)
               + jnp.sum(s, axis=1, keepdims=True))
        inv = 1.0 / jnp.maximum(jnp.abs(den), jnp.exp(-m_t))
        out_ref[:, h * DV_A:(h + 1) * DV_A] = num * inv

        wk = w_col * k.astype(F32)
        dec = decay8[r:r + 1, :]
        n_ref[r:r + 1, :] = dec * n_old + jnp.sum(wk, axis=0, keepdims=True)
        c_ref[r] = dec * c_old + _dot(wk.T.astype(BF16), v)

    m_ref[...] = jnp.broadcast_to(m_new8, m_ref.shape)


def _mlstm(qkv, gt):
    B, S, _ = qkv.shape
    L = CHUNK
    nc = S // L
    nqk = HA * DK_A
    nv = HA * DV_A

    def fwd(col):
        return lambda b, c: (b, c, col)

    def bwd(col):
        return lambda b, c: (b, nc - 1 - c, col)

    return pl.pallas_call(
        _mlstm_kernel,
        grid=(B, nc),
        in_specs=[
            pl.BlockSpec((None, L, nqk), fwd(0)),
            pl.BlockSpec((None, L, nqk), fwd(1)),
            pl.BlockSpec((None, L, nv), fwd(1)),
            pl.BlockSpec((None, L, nqk), bwd(0)),
            pl.BlockSpec((None, L, nqk), bwd(1)),
            pl.BlockSpec((None, L, nv), bwd(1)),
            pl.BlockSpec((None, 2 * HA, L), lambda b, c: (b, 0, c)),
            pl.BlockSpec((None, 2 * HA, L), lambda b, c: (b, 1, nc - 1 - c)),
        ],
        out_specs=[
            pl.BlockSpec((None, L, nv), fwd(0)),
            pl.BlockSpec((None, L, nv), bwd(0)),
        ],
        out_shape=[jax.ShapeDtypeStruct((B, S, nv), F32)] * 2,
        scratch_shapes=[
            pltpu.VMEM((2 * HA, DK_A, DV_A), F32),
            pltpu.VMEM((2 * HA, DK_A), F32),
            pltpu.VMEM((2 * HA, 128), F32),
            pltpu.VMEM((128, L), F32),
        ],
        compiler_params=_params("parallel", "arbitrary"),
        name="mlstm",
    )(qkv, qkv, qkv, qkv, qkv, qkv, gt, gt)


def _post_a_kernel(hf_ref, hb_ref, o_ref, x_ref, ng_ref, w_ref, out_ref):
    parts = []
    for h in range(HA):
        sl = slice(h * DV_A, (h + 1) * DV_A)
        hh = hf_ref[:, sl] + hb_ref[:, sl]
        hn = hh * lax.rsqrt(jnp.mean(hh * hh, axis=-1, keepdims=True) + EPS)
        hn = hn * ng_ref[:, sl] * jax.nn.sigmoid(o_ref[:, sl])
        parts.append(hn.astype(BF16))
    out_ref[...] = x_ref[...] + _dot(jnp.concatenate(parts, axis=1), w_ref[...])


def _post_a(hf, hb, o, x, norm_g, w_out):
    B, S, D = x.shape
    nv = hf.shape[-1]
    ts = min(TOKEN_TILE, S)
    tile = lambda n: pl.BlockSpec((None, ts, n), lambda b, i: (b, i, 0))
    return pl.pallas_call(
        _post_a_kernel,
        grid=(B, S // ts),
        in_specs=[tile(nv), tile(nv), tile(nv), tile(D), _resident((1, nv)), _resident(w_out.shape)],
        out_specs=tile(D),
        out_shape=jax.ShapeDtypeStruct((B, S, D), F32),
        compiler_params=_params("parallel", "parallel"),
        name="post_a",
    )(hf, hb, o, x, norm_g, w_out)


def _proj_b_kernel(x_ref, g_ref, wqt_ref, wk_ref, wvt_ref, qt_ref, k_ref, vt_ref):
    xn = _rms(x_ref[...], g_ref[...]).astype(BF16)
    qt = lax.dot_general(wqt_ref[...], xn, NT_DIMS, preferred_element_type=F32)
    qt_ref[...] = (qt * (DH_B ** -0.5 * LOG2E)).astype(BF16)
    vt = lax.dot_general(wvt_ref[...], xn, NT_DIMS, preferred_element_type=F32)
    vt_ref[...] = vt.astype(BF16)
    k_ref[...] = _dot(xn, wk_ref[...]).astype(BF16)


def _proj_b(x, g, w_q_t, w_k, w_v_t, t):
    B, S, D = x.shape
    nb = S // t
    n = w_k.shape[1]
    return pl.pallas_call(
        _proj_b_kernel,
        grid=(B, nb),
        in_specs=[
            pl.BlockSpec((None, t, D), lambda b, i: (b, i, 0)),
            _resident((1, D)),
            _resident(w_q_t.shape),
            _resident(w_k.shape),
            _resident(w_v_t.shape),
        ],
        out_specs=[
            pl.BlockSpec((None, None, n, t), lambda b, i: (b, i, 0, 0)),
            pl.BlockSpec((None, t, n), lambda b, i: (b, i, 0)),
            pl.BlockSpec((None, None, n, t), lambda b, i: (b, i, 0, 0)),
        ],
        out_shape=[
            jax.ShapeDtypeStruct((B, nb, n, t), BF16),
            jax.ShapeDtypeStruct((B, S, n), BF16),
            jax.ShapeDtypeStruct((B, nb, n, t), BF16),
        ],
        compiler_params=_params("parallel", "parallel"),
        name="proj_b",
    )(x, g, w_q_t, w_k, w_v_t)


def _bias_kernel(tab_ref, out_ref, *, t):
    h = pl.program_id(0)
    off = (pl.program_id(1) - 1) * t
    ki = lax.broadcasted_iota(jnp.int32, (t, t), 0)
    qi = lax.broadcasted_iota(jnp.int32, (t, t), 1)
    rp = off + ki - qi
    half = N_BUCKETS // 2
    max_exact = half // 2
    ret = (rp > 0).astype(jnp.int32) * half
    n = jnp.abs(rp)
    nf = jnp.maximum(n, 1).astype(F32)
    large = max_exact + (jnp.log(nf / max_exact) / math.log(MAX_DIST / max_exact)
                         * (half - max_exact)).astype(jnp.int32)
    large = jnp.minimum(large, half - 1)
    bucket = ret + jnp.where(n < max_exact, n, large)
    acc = jnp.zeros((t, t), F32)
    for b in range(N_BUCKETS):
        acc = jnp.where(bucket == b, tab_ref[b, h], acc)
    out_ref[...] = acc * LOG2E


def _bias_tiles(rel_bias, t):
    return pl.pallas_call(
        functools.partial(_bias_kernel, t=t),
        grid=(HB, 3),
        in_specs=[pl.BlockSpec(memory_space=pltpu.SMEM)],
        out_specs=pl.BlockSpec((None, None, t, t), lambda h, j: (h, j, 0, 0)),
        out_shape=jax.ShapeDtypeStruct((HB, 3, t, t), F32),
        compiler_params=_params("parallel", "parallel"),
        name="bias_tiles",
    )(rel_bias)


def _attn_kernel(qt_ref, k_ref, vt_ref, bias_ref, cfar_ref, lam_ref, g_ref, o_ref,
                 q1_ref, q2_ref, m_ref, l_ref, acc_ref, sa_ref, sb_ref, mba_ref, mbb_ref,
                 *, t, nk, lambda_init):
    h = pl.program_id(1)
    qi = pl.program_id(2)

    qt = qt_ref[...]
    rows = lax.broadcasted_iota(jnp.int32, qt.shape, 0)
    q1_ref[...] = jnp.where(rows < DH_B, qt, jnp.zeros_like(qt))
    q2_ref[...] = jnp.where(rows >= DH_B, qt, jnp.zeros_like(qt))
    m_ref[...] = jnp.full(m_ref.shape, -jnp.inf, F32)
    l_ref[...] = jnp.zeros(l_ref.shape, F32)
    acc_ref[...] = jnp.zeros(acc_ref.shape, F32)
    c_left = cfar_ref[h, 0] * LOG2E
    c_right = cfar_ref[h, 1] * LOG2E

    def is_near(kj):
        return jnp.abs(kj - qi) <= 1

    def scores(kj, s_ref, mb_ref):
        ks = pl.multiple_of(kj * t, t)
        kb = k_ref[pl.ds(ks, t), :]
        for j, q_ref in enumerate((q1_ref, q2_ref)):
            s = _dot(kb, q_ref[...])
            s_ref[j] = s
            mb_ref[j:j + 1, :] = jnp.max(s, axis=0, keepdims=True)

    def add_near_bias(kj, s_ref, mb_ref):
        @pl.when(is_near(kj))
        def _():
            bt = bias_ref[kj - qi + 1]
            for j in range(2):
                s = s_ref[j] + bt
                s_ref[j] = s
                mb_ref[j:j + 1, :] = jnp.max(s, axis=0, keepdims=True)

    def consume(kj, s_ref, mb_ref):
        c = jnp.where(is_near(kj), 0.0, jnp.where(kj < qi, c_left, c_right))
        vt = vt_ref[kj]
        for j in range(2):
            m_old = m_ref[j:j + 1, :]
            m_new = jnp.maximum(m_old, mb_ref[j:j + 1, :] + c)
            p = jnp.exp2(s_ref[j] - (m_new - c))
            alpha = jnp.exp2(m_old - m_new)
            l_ref[j:j + 1, :] = alpha * l_ref[j:j + 1, :] + jnp.sum(p, axis=0, keepdims=True)
            acc_ref[j] = alpha * acc_ref[j] + _dot(vt, p.astype(BF16))
            m_ref[j:j + 1, :] = m_new

    def pair(kj, last):
        add_near_bias(kj, sa_ref, mba_ref)
        scores(kj + 1, sb_ref, mbb_ref)
        consume(kj, sa_ref, mba_ref)
        add_near_bias(kj + 1, sb_ref, mbb_ref)
        if not last:
            scores(kj + 2, sa_ref, mba_ref)
        consume(kj + 1, sb_ref, mbb_ref)

    scores(0, sa_ref, mba_ref)

    def body(i, carry):
        pair(2 * i, False)
        return carry

    lax.fori_loop(0, nk // 2 - 1, body, 0)
    pair(nk - 2, True)

    lv = lam_ref[...]
    lam = (jnp.exp(jnp.sum(lv[0:1] * lv[1:2], axis=1, keepdims=True))
           - jnp.exp(jnp.sum(lv[2:3] * lv[3:4], axis=1, keepdims=True)) + lambda_init)
    o = (acc_ref[0] * (1.0 / l_ref[0:1, :])
         - lam * (acc_ref[1] * (1.0 / l_ref[1:2, :])))
    on = o * lax.rsqrt(jnp.mean(o * o, axis=0, keepdims=True) + EPS) * g_ref[...]
    on = on * (1.0 - lambda_init)
    o_ref[...] = on.T.astype(BF16)


def _attn(qt, k, vt, bias, cfar, lam_vecs, subln_g, lambda_init):
    B, nb, n, t = qt.shape
    S = k.shape[1]
    assert nb % 2 == 0, nb
    dh2 = 2 * DH_B
    return pl.pallas_call(
        functools.partial(_attn_kernel, t=t, nk=nb, lambda_init=lambda_init),
        grid=(B, HB, nb),
        in_specs=[
            pl.BlockSpec((None, None, dh2, t), lambda b, h, i: (b, i, h, 0)),
            pl.BlockSpec((None, S, dh2), lambda b, h, i: (b, 0, h)),
            pl.BlockSpec((None, nb, dh2, t), lambda b, h, i: (b, 0, h, 0)),
            pl.BlockSpec((None, 3, t, t), lambda b, h, i: (h, 0, 0, 0)),
            pl.BlockSpec(memory_space=pltpu.SMEM),
            _resident(lam_vecs.shape),
            _resident(subln_g.shape),
        ],
        out_specs=pl.BlockSpec((None, t, dh2), lambda b, h, i: (b, i, h)),
        out_shape=jax.ShapeDtypeStruct((B, S, n), BF16),
        scratch_shapes=[
            pltpu.VMEM((dh2, t), BF16),
            pltpu.VMEM((dh2, t), BF16),
            pltpu.VMEM((2, t), F32),
            pltpu.VMEM((2, t), F32),
            pltpu.VMEM((2, dh2, t), F32),
            pltpu.VMEM((2, t, t), F32),
            pltpu.VMEM((2, t, t), F32),
            pltpu.VMEM((2, t), F32),
            pltpu.VMEM((2, t), F32),
        ],
        compiler_params=_params("parallel", "parallel", "parallel"),
        name="diff_attn",
    )(qt, k, vt, bias, cfar, lam_vecs, subln_g)


def _out_b_kernel(o_ref, x_ref, w_ref, out_ref):
    out_ref[...] = x_ref[...] + _dot(o_ref[...], w_ref[...])


def _out_b(o, x, w_out):
    B, S, D = x.shape
    ts = min(TOKEN_TILE, S)
    tile = lambda n: pl.BlockSpec((None, ts, n), lambda b, i: (b, i, 0))
    return pl.pallas_call(
        _out_b_kernel,
        grid=(B, S // ts),
        in_specs=[tile(o.shape[-1]), tile(D), _resident(w_out.shape)],
        out_specs=tile(D),
        out_shape=jax.ShapeDtypeStruct((B, S, D), F32),
        compiler_params=_params("parallel", "parallel"),
        name="out_b",
    )(o, x, w_out)


def _mem_kv_kernel(mem_ref, g_ref, w_ref, kv_ref):
    mn = _rms(mem_ref[...], g_ref[...]).astype(BF16)
    kv_ref[...] = _dot(mn, w_ref[...]).astype(BF16)


def _mem_kv(mem, g, w_kv):
    B, M, D = mem.shape
    n = w_kv.shape[1]
    return pl.pallas_call(
        _mem_kv_kernel,
        grid=(B,),
        in_specs=[pl.BlockSpec((None, M, D), lambda b: (b, 0, 0)), _resident((1, D)),
                  _resident(w_kv.shape)],
        out_specs=pl.BlockSpec((None, M, n), lambda b: (b, 0, 0)),
        out_shape=jax.ShapeDtypeStruct((B, M, n), BF16),
        compiler_params=_params("parallel"),
        name="mem_kv",
    )(mem, g, w_kv)


def _cross_kernel(x_ref, g_ref, wq_ref, kv_ref, wo_ref, out_ref):
    x = x_ref[...]
    D = x.shape[-1]
    dc = D // HC
    xn = _rms(x, g_ref[...]).astype(BF16)
    q = _dot(xn, wq_ref[...])
    parts = []
    for h in range(HC):
        qh = (q[:, h * dc:(h + 1) * dc] * (dc ** -0.5)).astype(BF16)
        kh = kv_ref[:, h * dc:(h + 1) * dc]
        vh = kv_ref[:, D + h * dc:D + (h + 1) * dc]
        s = lax.dot_general(qh, kh, NT_DIMS, preferred_element_type=F32)
        p = jnp.exp(s - jnp.max(s, axis=-1, keepdims=True))
        inv = 1.0 / jnp.sum(p, axis=-1, keepdims=True)
        parts.append((_dot(p.astype(BF16), vh) * inv).astype(BF16))
    out_ref[...] = x + _dot(jnp.concatenate(parts, axis=1), wo_ref[...])


def _cross(x, g, w_q, kv, w_out):
    B, S, D = x.shape
    M = kv.shape[1]
    ts = min(TOKEN_TILE, S)
    tile = pl.BlockSpec((None, ts, D), lambda b, i: (b, i, 0))
    return pl.pallas_call(
        _cross_kernel,
        grid=(B, S // ts),
        in_specs=[tile, _resident((1, D)), _resident(w_q.shape),
                  pl.BlockSpec((None, M, 2 * D), lambda b, i: (b, 0, 0)),
                  _resident(w_out.shape)],
        out_specs=tile,
        out_shape=jax.ShapeDtypeStruct((B, S, D), F32),
        compiler_params=_params("parallel", "parallel"),
        name="cross_attn",
    )(x, g, w_q, kv, w_out)


def _mlp_kernel(x_ref, g_ref, w1_ref, w2_ref, gf_ref, out_ref, *, final_norm):
    x = x_ref[...]
    D = x.shape[-1]
    xn = _rms(x, g_ref[...]).astype(BF16)
    acc = x
    for c in range(w1_ref.shape[1] // D):
        hcol = _dot(xn, w1_ref[:, c * D:(c + 1) * D])
        hcol = jnp.square(jnp.maximum(hcol, 0.0)).astype(BF16)
        acc = acc + _dot(hcol, w2_ref[c * D:(c + 1) * D, :])
    out_ref[...] = _rms(acc, gf_ref[...]) if final_norm else acc


def _mlp(x, g, w1, w2, g_final, final_norm):
    B, S, D = x.shape
    ts = min(TOKEN_TILE, S)
    tile = pl.BlockSpec((None, ts, D), lambda b, i: (b, i, 0))
    return pl.pallas_call(
        functools.partial(_mlp_kernel, final_norm=final_norm),
        grid=(B, S // ts),
        in_specs=[tile, _resident((1, D)), _resident(w1.shape), _resident(w2.shape),
                  _resident((1, D))],
        out_specs=tile,
        out_shape=jax.ShapeDtypeStruct((B, S, D), F32),
        compiler_params=_params("parallel", "parallel"),
        name="mlp",
    )(x, g, w1, w2, g_final)


def _prepare(g_mix, g_cross, g_mem, g_mlp, g_final, a_w_in, a_w_gate, a_b_gate, a_norm_g,
             a_w_out, b_w_qkv, b_lambda, b_subln_g, b_w_out, rel_bias, c_w_q, c_w_kv,
             c_w_out, f_w1, f_w2):
    row = lambda v: v.reshape(1, -1).astype(F32)
    nq = 2 * HB * DH_B
    wqkv = b_w_qkv[0]
    return dict(
        g_mix=[row(g) for g in g_mix], g_cross=[row(g) for g in g_cross],
        g_mem=[row(g) for g in g_mem], g_mlp=[row(g) for g in g_mlp], g_final=row(g_final),
        a_w_in=a_w_in[0].astype(BF16), a_w_gate_t=a_w_gate[0].T.astype(BF16),
        a_b_gate=a_b_gate[0].reshape(-1, 1).astype(F32), a_norm_g=row(a_norm_g[0]),
        a_w_out=a_w_out[0].astype(BF16),
        b_w_q_t=wqkv[:, :nq].T.astype(BF16), b_w_k=wqkv[:, nq:2 * nq].astype(BF16),
        b_w_v_t=wqkv[:, 2 * nq:].T.astype(BF16),
        b_lambda=b_lambda[0].astype(F32), b_subln_g=b_subln_g[0].reshape(-1, 1).astype(F32),
        b_w_out=b_w_out[0].astype(BF16),
        rel_bias=rel_bias.astype(F32),
        cfar=jnp.stack([rel_bias[N_BUCKETS // 2 - 1], rel_bias[N_BUCKETS - 1]], axis=1).astype(F32),
        c_w_q=[w.astype(BF16) for w in c_w_q], c_w_kv=[w.astype(BF16) for w in c_w_kv],
        c_w_out=[w.astype(BF16) for w in c_w_out],
        f_w1=[w.astype(BF16) for w in f_w1], f_w2=[w.astype(BF16) for w in f_w2],
    )


def _encoder(x, mem, p, bias, t):
    depth = len(p["g_mix"])
    for i in range(depth):
        if i % 2 == 0:
            qkv, o, gt = _proj_a(x, p["g_mix"][i], p["a_w_in"], p["a_w_gate_t"], p["a_b_gate"])
            hf, hb = _mlstm(qkv, gt)
            x = _post_a(hf, hb, o, x, p["a_norm_g"], p["a_w_out"])
        else:
            qt, k, vt = _proj_b(x, p["g_mix"][i], p["b_w_q_t"], p["b_w_k"], p["b_w_v_t"], t)
            o = _attn(qt, k, vt, bias, p["cfar"], p["b_lambda"], p["b_subln_g"], _lambda_init(i))
            x = _out_b(o, x, p["b_w_out"])
        kv = _mem_kv(mem, p["g_mem"][i], p["c_w_kv"][i])
        x = _cross(x, p["g_cross"][i], p["c_w_q"][i], kv, p["c_w_out"][i])
        x = _mlp(x, p["g_mlp"][i], p["f_w1"][i], p["f_w2"][i], p["g_final"], i == depth - 1)
    return x


def kernel(x_prompt, x_sample, mem_prompt, mem_sample, g_mix, g_cross, g_mem, g_mlp, g_final, a_w_in, a_w_gate, a_b_gate, a_norm_g, a_w_out, b_w_qkv, b_lambda, b_subln_g, b_w_out, rel_bias, c_w_q, c_w_kv, c_w_out, f_w1, f_w2):
    p = _prepare(g_mix, g_cross, g_mem, g_mlp, g_final, a_w_in, a_w_gate, a_b_gate, a_norm_g,
                 a_w_out, b_w_qkv, b_lambda, b_subln_g, b_w_out, rel_bias, c_w_q, c_w_kv,
                 c_w_out, f_w1, f_w2)
    t = min(TOKEN_TILE, x_prompt.shape[1], x_sample.shape[1])
    bias = _bias_tiles(p["rel_bias"], t)
    y_prompt = _encoder(x_prompt, mem_prompt, p, bias, t)
    y_sample = _encoder(x_sample, mem_sample, p, bias, t)
    return (y_prompt, y_sample)
```

```python
import functools
import math

import jax
import jax.numpy as jnp
from jax import lax
from jax.experimental import pallas as pl
from jax.experimental.pallas import tpu as pltpu

EPS = 1e-6
HA = 4
DK_A = 128
DV_A = 256
CHUNK = 128
HB = 8
DH_B = 64
N_BUCKETS = 32
MAX_DIST = 128
HC = 4

F32 = jnp.float32
BF16 = jnp.bfloat16

VMEM_LIMIT_BYTES = 56 * 1024 * 1024
TOKEN_TILE = 512
NT_DIMS = (((1,), (1,)), ((), ()))
LOG2E = math.log2(math.e)
V_ROWS = 2 * DH_B + 16
NEAR_STEPS = 3
FAR_UNROLL = 2


def _lambda_init(layer):
    return 0.8 - 0.6 * math.exp(-0.3 * layer)


def _params(*sem):
    return pltpu.CompilerParams(dimension_semantics=sem, vmem_limit_bytes=VMEM_LIMIT_BYTES)


def _resident(shape):
    zeros = (0,) * len(shape)
    return pl.BlockSpec(shape, lambda *_: zeros, pipeline_mode=pl.Buffered(1))


def _rms(x, g):
    return x * lax.rsqrt(jnp.mean(x * x, axis=-1, keepdims=True) + EPS) * g


def _dot(a, b):
    return jnp.dot(a, b, preferred_element_type=F32)


def _proj_a_kernel(x_ref, g_ref, w_ref, wgt_ref, bg_ref, qkv_ref, o_ref, gt_ref):
    xn = _rms(x_ref[...], g_ref[...]).astype(BF16)
    nqk = HA * DK_A
    for c in range(4):
        y = _dot(xn, w_ref[:, c * nqk:(c + 1) * nqk])
        if c == 1:
            y = y * (DK_A ** -0.5)
        qkv_ref[:, c * nqk:(c + 1) * nqk] = y.astype(BF16)
    for c in range(2):
        o_ref[:, c * nqk:(c + 1) * nqk] = _dot(xn, w_ref[:, (4 + c) * nqk:(5 + c) * nqk])
    gt = lax.dot_general(wgt_ref[...], xn, NT_DIMS, preferred_element_type=F32)
    gt_ref[...] = gt + bg_ref[...]


def _proj_a(x, g, w_in, w_gate_t, b_gate):
    B, S, D = x.shape
    ts = min(TOKEN_TILE, S)
    n_qkv = 2 * HA * DK_A + HA * DV_A
    n_o = HA * DV_A
    return pl.pallas_call(
        _proj_a_kernel,
        grid=(B, S // ts),
        in_specs=[
            pl.BlockSpec((None, ts, D), lambda b, i: (b, i, 0)),
            _resident((1, D)),
            _resident(w_in.shape),
            _resident(w_gate_t.shape),
            _resident(b_gate.shape),
        ],
        out_specs=[
            pl.BlockSpec((None, ts, n_qkv), lambda b, i: (b, i, 0)),
            pl.BlockSpec((None, ts, n_o), lambda b, i: (b, i, 0)),
            pl.BlockSpec((None, 4 * HA, ts), lambda b, i: (b, 0, i)),
        ],
        out_shape=[
            jax.ShapeDtypeStruct((B, S, n_qkv), BF16),
            jax.ShapeDtypeStruct((B, S, n_o), F32),
            jax.ShapeDtypeStruct((B, 4 * HA, S), F32),
        ],
        compiler_params=_params("parallel", "parallel"),
        name="proj_a",
    )(x, g, w_in, w_gate_t, b_gate)


def _log_sigmoid(x):
    return jnp.minimum(x, 0.0) - jnp.log1p(jnp.exp(-jnp.abs(x)))


def _mlstm_kernel(qf_ref, kf_ref, vf_ref, qb_ref, kb_ref, vb_ref, gf_ref, gb_ref,
                  hf_ref, hb_ref, c_ref, n_ref, m_ref, t_ref):
    L = CHUNK

    @pl.when(pl.program_id(1) == 0)
    def _():
        c_ref[...] = jnp.zeros(c_ref.shape, F32)
        n_ref[...] = jnp.zeros(n_ref.shape, F32)
        m_ref[...] = jnp.zeros(m_ref.shape, F32)
        t_ref[...] = jnp.zeros(t_ref.shape, F32)

    gf = gf_ref[...]
    gb = gb_ref[...]
    row8 = lax.broadcasted_iota(jnp.int32, (2 * HA, L), 0)
    lane8 = lax.broadcasted_iota(jnp.int32, (2 * HA, L), 1)
    is_f = row8 < HA
    ig8 = jnp.where(is_f, gf, pltpu.roll(gb, HA, 0))
    fg8 = jnp.where(is_f, pltpu.roll(gf, HA, 0), gb)
    lf8 = _log_sigmoid(fg8)

    b8 = lf8
    d = 1
    while d < L:
        fwd = jnp.where(lane8 >= d, pltpu.roll(b8, d, 1), 0.0)
        bwd = jnp.where(lane8 < L - d, pltpu.roll(b8, L - d, 1), 0.0)
        b8 = b8 + jnp.where(is_f, fwd, bwd)
        d *= 2

    bl8 = jnp.where(is_f[:, 0:1], b8[:, L - 1:L], b8[:, 0:1])
    m_old8 = m_ref[:, 0:1]
    a8 = ig8 - b8
    wlog8 = bl8 + a8
    m_new8 = jnp.maximum(bl8 + m_old8, jnp.max(wlog8, axis=1, keepdims=True))
    w8 = jnp.exp(wlog8 - m_new8)
    decay8 = jnp.exp(bl8 + m_old8 - m_new8)

    t_ref[0:2 * HA, :] = b8
    t_ref[2 * HA:4 * HA, :] = w8
    tt = t_ref[...].T

    ti = lax.broadcasted_iota(jnp.int32, (L, L), 0)
    si = lax.broadcasted_iota(jnp.int32, (L, L), 1)

    for r in range(2 * HA):
        h = r % HA
        if r < HA:
            q_ref, k_ref, v_ref, out_ref, mask = qf_ref, kf_ref, vf_ref, hf_ref, si <= ti
        else:
            q_ref, k_ref, v_ref, out_ref, mask = qb_ref, kb_ref, vb_ref, hb_ref, si >= ti
        q = q_ref[:, h * DK_A:(h + 1) * DK_A]
        k = k_ref[:, h * DK_A:(h + 1) * DK_A]
        v = v_ref[:, h * DV_A:(h + 1) * DV_A]
        b_col = tt[:, r:r + 1]
        w_col = tt[:, 2 * HA + r:2 * HA + r + 1]
        m_old = m_old8[r:r + 1, :]

        dlog = jnp.where(mask, b_col + a8[r:r + 1, :], -jnp.inf)
        inter = b_col + m_old
        m_t = jnp.maximum(inter, jnp.max(dlog, axis=1, keepdims=True))
        s = lax.dot_general(q, k, NT_DIMS, preferred_element_type=F32) * jnp.exp(dlog - m_t)
        iw = jnp.exp(inter - m_t)
        c_old = c_ref[r]
        n_old = n_ref[r:r + 1, :]
        num = iw * _dot(q, c_old.astype(BF16)) + _dot(s.astype(BF16), v)
        den = (iw * jnp.sum(q.astype(F32) * n_old, axis=1, keepdims=True)
               + jnp.sum(s, axis=1, keepdims=True))
        inv = 1.0 / jnp.maximum(jnp.abs(den), jnp.exp(-m_t))
        out_ref[:, h * DV_A:(h + 1) * DV_A] = num * inv

        wk = w_col * k.astype(F32)
        dec = decay8[r:r + 1, :]
        n_ref[r:r + 1, :] = dec * n_old + jnp.sum(wk, axis=0, keepdims=True)
        c_ref[r] = dec * c_old + _dot(wk.T.astype(BF16), v)

    m_ref[...] = jnp.broadcast_to(m_new8, m_ref.shape)


def _mlstm(qkv, gt):
    B, S, _ = qkv.shape
    L = CHUNK
    nc = S // L
    nqk = HA * DK_A
    nv = HA * DV_A

    def fwd(col):
        return lambda b, c: (b, c, col)

    def bwd(col):
        return lambda b, c: (b, nc - 1 - c, col)

    return pl.pallas_call(
        _mlstm_kernel,
        grid=(B, nc),
        in_specs=[
            pl.BlockSpec((None, L, nqk), fwd(0)),
            pl.BlockSpec((None, L, nqk), fwd(1)),
            pl.BlockSpec((None, L, nv), fwd(1)),
            pl.BlockSpec((None, L, nqk), bwd(0)),
            pl.BlockSpec((None, L, nqk), bwd(1)),
            pl.BlockSpec((None, L, nv), bwd(1)),
            pl.BlockSpec((None, 2 * HA, L), lambda b, c: (b, 0, c)),
            pl.BlockSpec((None, 2 * HA, L), lambda b, c: (b, 1, nc - 1 - c)),
        ],
        out_specs=[
            pl.BlockSpec((None, L, nv), fwd(0)),
            pl.BlockSpec((None, L, nv), bwd(0)),
        ],
        out_shape=[jax.ShapeDtypeStruct((B, S, nv), F32)] * 2,
        scratch_shapes=[
            pltpu.VMEM((2 * HA, DK_A, DV_A), F32),
            pltpu.VMEM((2 * HA, DK_A), F32),
            pltpu.VMEM((2 * HA, 128), F32),
            pltpu.VMEM((128, L), F32),
        ],
        compiler_params=_params("parallel", "arbitrary"),
        name="mlstm",
    )(qkv, qkv, qkv, qkv, qkv, qkv, gt, gt)


def _post_a_kernel(hf_ref, hb_ref, o_ref, x_ref, ng_ref, w_ref, out_ref):
    parts = []
    for h in range(HA):
        sl = slice(h * DV_A, (h + 1) * DV_A)
        hh = hf_ref[:, sl] + hb_ref[:, sl]
        hn = hh * lax.rsqrt(jnp.mean(hh * hh, axis=-1, keepdims=True) + EPS)
        hn = hn * ng_ref[:, sl] * jax.nn.sigmoid(o_ref[:, sl])
        parts.append(hn.astype(BF16))
    out_ref[...] = x_ref[...] + _dot(jnp.concatenate(parts, axis=1), w_ref[...])


def _post_a(hf, hb, o, x, norm_g, w_out):
    B, S, D = x.shape
    nv = hf.shape[-1]
    ts = min(TOKEN_TILE, S)
    tile = lambda n: pl.BlockSpec((None, ts, n), lambda b, i: (b, i, 0))
    return pl.pallas_call(
        _post_a_kernel,
        grid=(B, S // ts),
        in_specs=[tile(nv), tile(nv), tile(nv), tile(D), _resident((1, nv)), _resident(w_out.shape)],
        out_specs=tile(D),
        out_shape=jax.ShapeDtypeStruct((B, S, D), F32),
        compiler_params=_params("parallel", "parallel"),
        name="post_a",
    )(hf, hb, o, x, norm_g, w_out)


def _proj_b_kernel(x_ref, g_ref, wqt_ref, wk_ref, wvt_ref, qt_ref, k_ref, vt_ref):
    xn = _rms(x_ref[...], g_ref[...]).astype(BF16)
    qt = lax.dot_general(wqt_ref[...], xn, NT_DIMS, preferred_element_type=F32)
    qt_ref[...] = (qt * (DH_B ** -0.5 * LOG2E)).astype(BF16)
    vt = lax.dot_general(wvt_ref[...], xn, NT_DIMS, preferred_element_type=F32)
    dv = 2 * DH_B
    for h in range(HB):
        vt_ref[h, 0:dv, :] = vt[h * dv:(h + 1) * dv].astype(BF16)
        vt_ref[h, dv:, :] = jnp.ones((V_ROWS - dv, vt.shape[1]), BF16)
    k_ref[...] = _dot(xn, wk_ref[...]).astype(BF16)


def _proj_b(x, g, w_q_t, w_k, w_v_t, t):
    B, S, D = x.shape
    nb = S // t
    n = w_k.shape[1]
    return pl.pallas_call(
        _proj_b_kernel,
        grid=(B, nb),
        in_specs=[
            pl.BlockSpec((None, t, D), lambda b, i: (b, i, 0)),
            _resident((1, D)),
            _resident(w_q_t.shape),
            _resident(w_k.shape),
            _resident(w_v_t.shape),
        ],
        out_specs=[
            pl.BlockSpec((None, None, n, t), lambda b, i: (b, i, 0, 0)),
            pl.BlockSpec((None, t, n), lambda b, i: (b, i, 0)),
            pl.BlockSpec((None, None, HB, V_ROWS, t), lambda b, i: (b, i, 0, 0, 0)),
        ],
        out_shape=[
            jax.ShapeDtypeStruct((B, nb, n, t), BF16),
            jax.ShapeDtypeStruct((B, S, n), BF16),
            jax.ShapeDtypeStruct((B, nb, HB, V_ROWS, t), BF16),
        ],
        compiler_params=_params("parallel", "parallel"),
        name="proj_b",
    )(x, g, w_q_t, w_k, w_v_t)


def _bias_kernel(tab_ref, out_ref, *, t):
    h = pl.program_id(0)
    off = (pl.program_id(1) - 1) * t
    ki = lax.broadcasted_iota(jnp.int32, (t, t), 0)
    qi = lax.broadcasted_iota(jnp.int32, (t, t), 1)
    rp = off + ki - qi
    half = N_BUCKETS // 2
    max_exact = half // 2
    ret = (rp > 0).astype(jnp.int32) * half
    n = jnp.abs(rp)
    nf = jnp.maximum(n, 1).astype(F32)
    large = max_exact + (jnp.log(nf / max_exact) / math.log(MAX_DIST / max_exact)
                         * (half - max_exact)).astype(jnp.int32)
    large = jnp.minimum(large, half - 1)
    bucket = ret + jnp.where(n < max_exact, n, large)
    acc = jnp.zeros((t, t), F32)
    for b in range(N_BUCKETS):
        acc = jnp.where(bucket == b, tab_ref[b, h], acc)
    out_ref[...] = acc * LOG2E


def _bias_tiles(rel_bias, t):
    return pl.pallas_call(
        functools.partial(_bias_kernel, t=t),
        grid=(HB, 3),
        in_specs=[pl.BlockSpec(memory_space=pltpu.SMEM)],
        out_specs=pl.BlockSpec((None, None, t, t), lambda h, j: (h, j, 0, 0)),
        out_shape=jax.ShapeDtypeStruct((HB, 3, t, t), F32),
        compiler_params=_params("parallel", "parallel"),
        name="bias_tiles",
    )(rel_bias)


def _attn_kernel(qt_ref, k_ref, vt_ref, bias_ref, cfar_ref, lam_ref, g_ref, o_ref,
                 q1_ref, q2_ref, m_ref, acc_ref, s0_ref, s1_ref, p0_ref, p1_ref, mb_ref, al_ref,
                 *, t, nk, lambda_init):
    h = pl.program_id(1)
    qi = pl.program_id(2)
    s_refs = (s0_ref, s1_ref)
    p_refs = (p0_ref, p1_ref)
    dv = 2 * DH_B

    qt = qt_ref[...]
    rows = lax.broadcasted_iota(jnp.int32, qt.shape, 0)
    q1_ref[...] = jnp.where(rows < DH_B, qt, jnp.zeros_like(qt))
    q2_ref[...] = jnp.where(rows >= DH_B, qt, jnp.zeros_like(qt))
    m_ref[...] = jnp.full(m_ref.shape, -jnp.inf, F32)
    acc_ref[...] = jnp.zeros(acc_ref.shape, F32)
    c_left = cfar_ref[h, 0] * LOG2E
    c_right = cfar_ref[h, 1] * LOG2E

    def is_near(kj):
        return jnp.abs(kj - qi) <= 1

    def block_of(n):
        x = qi - 1 + n
        return jnp.where(x < 0, x + nk, jnp.where(x >= nk, x - nk, x))

    def stage_a(n, par):
        kj = block_of(n)
        ks = pl.multiple_of(kj * t, t)
        kb = k_ref[pl.ds(ks, t), :]
        for j, q_ref in enumerate((q1_ref, q2_ref)):
            s = _dot(kb, q_ref[...])
            s_refs[par][j] = s
            mb_ref[par, j:j + 1, :] = jnp.max(s, axis=0, keepdims=True)

    def add_near_bias(n, par):
        if n >= NEAR_STEPS:
            return
        kj = block_of(n)

        @pl.when(is_near(kj))
        def _():
            bt = bias_ref[kj - qi + 1]
            for j in range(2):
                s = s_refs[par][j] + bt
                s_refs[par][j] = s
                mb_ref[par, j:j + 1, :] = jnp.max(s, axis=0, keepdims=True)

    def stage_b(n, par, maybe_near):
        kj = block_of(n)
        c = jnp.where(kj < qi, c_left, c_right)
        if maybe_near:
            c = jnp.where(is_near(kj), 0.0, c)
        for j in range(2):
            m_old = m_ref[j:j + 1, :]
            m_new = jnp.maximum(m_old, mb_ref[par, j:j + 1, :] + c)
            p_refs[par][j] = jnp.exp2(s_refs[par][j] - (m_new - c)).astype(BF16)
            al_ref[par, j:j + 1, :] = jnp.exp2(m_old - m_new)
            m_ref[j:j + 1, :] = m_new

    def stage_c(n, par):
        vt = vt_ref[block_of(n)]
        for j in range(2):
            acc_ref[j] = al_ref[par, j:j + 1, :] * acc_ref[j] + _dot(vt, p_refs[par][j])

    def step(n, par, maybe_near):
        stage_a(n + 2, par)
        stage_b(n + 1, 1 - par, maybe_near)
        stage_c(n, par)

    stage_a(0, 0)
    add_near_bias(0, 0)
    stage_a(1, 1)
    stage_b(0, 0, True)
    for n in range(NEAR_STEPS - 1):
        add_near_bias(n + 1, (n + 1) % 2)
        step(n, n % 2, True)

    def far_steps(n0, count):
        for u in range(count):
            step(n0 + u, (NEAR_STEPS - 1 + u) % 2, False)

    def body(i, carry):
        far_steps(NEAR_STEPS - 1 + FAR_UNROLL * i, FAR_UNROLL)
        return carry

    n_far = nk - 1 - NEAR_STEPS
    lax.fori_loop(0, n_far // FAR_UNROLL, body, 0)
    far_steps(NEAR_STEPS - 1 + n_far - n_far % FAR_UNROLL, n_far % FAR_UNROLL)
    stage_b(nk - 1, (nk - 1) % 2, False)
    stage_c(nk - 2, nk % 2)
    stage_c(nk - 1, (nk - 1) % 2)

    lv = lam_ref[...]
    lam = (jnp.exp(jnp.sum(lv[0:1] * lv[1:2], axis=1, keepdims=True))
           - jnp.exp(jnp.sum(lv[2:3] * lv[3:4], axis=1, keepdims=True)) + lambda_init)
    o = (acc_ref[0, 0:dv, :] * (1.0 / acc_ref[0, dv:dv + 1, :])
         - lam * (acc_ref[1, 0:dv, :] * (1.0 / acc_ref[1, dv:dv + 1, :])))
    on = o * lax.rsqrt(jnp.mean(o * o, axis=0, keepdims=True) + EPS) * g_ref[...]
    on = on * (1.0 - lambda_init)
    o_ref[...] = on.T.astype(BF16)


def _attn(qt, k, vt, bias, cfar, lam_vecs, subln_g, lambda_init):
    B, nb, n, t = qt.shape
    S = k.shape[1]
    assert nb % 2 == 0 and nb >= 4, nb
    dh2 = 2 * DH_B
    return pl.pallas_call(
        functools.partial(_attn_kernel, t=t, nk=nb, lambda_init=lambda_init),
        grid=(B, HB, nb),
        in_specs=[
            pl.BlockSpec((None, None, dh2, t), lambda b, h, i: (b, i, h, 0)),
            pl.BlockSpec((None, S, dh2), lambda b, h, i: (b, 0, h)),
            pl.BlockSpec((None, nb, None, V_ROWS, t), lambda b, h, i: (b, 0, h, 0, 0)),
            pl.BlockSpec((None, 3, t, t), lambda b, h, i: (h, 0, 0, 0)),
            pl.BlockSpec(memory_space=pltpu.SMEM),
            _resident(lam_vecs.shape),
            _resident(subln_g.shape),
        ],
        out_specs=pl.BlockSpec((None, t, dh2), lambda b, h, i: (b, i, h)),
        out_shape=jax.ShapeDtypeStruct((B, S, n), BF16),
        scratch_shapes=[
            pltpu.VMEM((dh2, t), BF16),
            pltpu.VMEM((dh2, t), BF16),
            pltpu.VMEM((2, t), F32),
            pltpu.VMEM((2, V_ROWS, t), F32),
            pltpu.VMEM((2, t, t), F32),
            pltpu.VMEM((2, t, t), F32),
            pltpu.VMEM((2, t, t), BF16),
            pltpu.VMEM((2, t, t), BF16),
            pltpu.VMEM((2, 2, t), F32),
            pltpu.VMEM((2, 2, t), F32),
        ],
        compiler_params=_params("parallel", "parallel", "parallel"),
        name="diff_attn",
    )(qt, k, vt, bias, cfar, lam_vecs, subln_g)


def _out_b_kernel(o_ref, x_ref, w_ref, out_ref):
    out_ref[...] = x_ref[...] + _dot(o_ref[...], w_ref[...])


def _out_b(o, x, w_out):
    B, S, D = x.shape
    ts = min(TOKEN_TILE, S)
    tile = lambda n: pl.BlockSpec((None, ts, n), lambda b, i: (b, i, 0))
    return pl.pallas_call(
        _out_b_kernel,
        grid=(B, S // ts),
        in_specs=[tile(o.shape[-1]), tile(D), _resident(w_out.shape)],
        out_specs=tile(D),
        out_shape=jax.ShapeDtypeStruct((B, S, D), F32),
        compiler_params=_params("parallel", "parallel"),
        name="out_b",
    )(o, x, w_out)


def _mem_kv_kernel(mem_ref, g_ref, w_ref, kv_ref):
    mn = _rms(mem_ref[...], g_ref[...]).astype(BF16)
    kv_ref[...] = _dot(mn, w_ref[...]).astype(BF16)


def _mem_kv(mem, g, w_kv):
    B, M, D = mem.shape
    n = w_kv.shape[1]
    return pl.pallas_call(
        _mem_kv_kernel,
        grid=(B,),
        in_specs=[pl.BlockSpec((None, M, D), lambda b: (b, 0, 0)), _resident((1, D)),
                  _resident(w_kv.shape)],
        out_specs=pl.BlockSpec((None, M, n), lambda b: (b, 0, 0)),
        out_shape=jax.ShapeDtypeStruct((B, M, n), BF16),
        compiler_params=_params("parallel"),
        name="mem_kv",
    )(mem, g, w_kv)


def _cross_kernel(x_ref, g_ref, wq_ref, kv_ref, wo_ref, out_ref):
    x = x_ref[...]
    D = x.shape[-1]
    dc = D // HC
    xn = _rms(x, g_ref[...]).astype(BF16)
    q = _dot(xn, wq_ref[...])
    parts = []
    for h in range(HC):
        qh = (q[:, h * dc:(h + 1) * dc] * (dc ** -0.5)).astype(BF16)
        kh = kv_ref[:, h * dc:(h + 1) * dc]
        vh = kv_ref[:, D + h * dc:D + (h + 1) * dc]
        s = lax.dot_general(qh, kh, NT_DIMS, preferred_element_type=F32)
        p = jnp.exp(s - jnp.max(s, axis=-1, keepdims=True))
        inv = 1.0 / jnp.sum(p, axis=-1, keepdims=True)
        parts.append((_dot(p.astype(BF16), vh) * inv).astype(BF16))
    out_ref[...] = x + _dot(jnp.concatenate(parts, axis=1), wo_ref[...])


def _cross(x, g, w_q, kv, w_out):
    B, S, D = x.shape
    M = kv.shape[1]
    ts = min(TOKEN_TILE, S)
    tile = pl.BlockSpec((None, ts, D), lambda b, i: (b, i, 0))
    return pl.pallas_call(
        _cross_kernel,
        grid=(B, S // ts),
        in_specs=[tile, _resident((1, D)), _resident(w_q.shape),
                  pl.BlockSpec((None, M, 2 * D), lambda b, i: (b, 0, 0)),
                  _resident(w_out.shape)],
        out_specs=tile,
        out_shape=jax.ShapeDtypeStruct((B, S, D), F32),
        compiler_params=_params("parallel", "parallel"),
        name="cross_attn",
    )(x, g, w_q, kv, w_out)


def _mlp_kernel(x_ref, g_ref, w1_ref, w2_ref, gf_ref, out_ref, *, final_norm):
    x = x_ref[...]
    D = x.shape[-1]
    xn = _rms(x, g_ref[...]).astype(BF16)
    acc = x
    for c in range(w1_ref.shape[1] // D):
        hcol = _dot(xn, w1_ref[:, c * D:(c + 1) * D])
        hcol = jnp.square(jnp.maximum(hcol, 0.0)).astype(BF16)
        acc = acc + _dot(hcol, w2_ref[c * D:(c + 1) * D, :])
    out_ref[...] = _rms(acc, gf_ref[...]) if final_norm else acc


def _mlp(x, g, w1, w2, g_final, final_norm):
    B, S, D = x.shape
    ts = min(TOKEN_TILE, S)
    tile = pl.BlockSpec((None, ts, D), lambda b, i: (b, i, 0))
    return pl.pallas_call(
        functools.partial(_mlp_kernel, final_norm=final_norm),
        grid=(B, S // ts),
        in_specs=[tile, _resident((1, D)), _resident(w1.shape), _resident(w2.shape),
                  _resident((1, D))],
        out_specs=tile,
        out_shape=jax.ShapeDtypeStruct((B, S, D), F32),
        compiler_params=_params("parallel", "parallel"),
        name="mlp",
    )(x, g, w1, w2, g_final)


def _prepare(g_mix, g_cross, g_mem, g_mlp, g_final, a_w_in, a_w_gate, a_b_gate, a_norm_g,
             a_w_out, b_w_qkv, b_lambda, b_subln_g, b_w_out, rel_bias, c_w_q, c_w_kv,
             c_w_out, f_w1, f_w2):
    row = lambda v: v.reshape(1, -1).astype(F32)
    nq = 2 * HB * DH_B
    wqkv = b_w_qkv[0]
    return dict(
        g_mix=[row(g) for g in g_mix], g_cross=[row(g) for g in g_cross],
        g_mem=[row(g) for g in g_mem], g_mlp=[row(g) for g in g_mlp], g_final=row(g_final),
        a_w_in=a_w_in[0].astype(BF16), a_w_gate_t=a_w_gate[0].T.astype(BF16),
        a_b_gate=a_b_gate[0].reshape(-1, 1).astype(F32), a_norm_g=row(a_norm_g[0]),
        a_w_out=a_w_out[0].astype(BF16),
        b_w_q_t=wqkv[:, :nq].T.astype(BF16), b_w_k=wqkv[:, nq:2 * nq].astype(BF16),
        b_w_v_t=wqkv[:, 2 * nq:].T.astype(BF16),
        b_lambda=b_lambda[0].astype(F32), b_subln_g=b_subln_g[0].reshape(-1, 1).astype(F32),
        b_w_out=b_w_out[0].astype(BF16),
        rel_bias=rel_bias.astype(F32),
        cfar=jnp.stack([rel_bias[N_BUCKETS // 2 - 1], rel_bias[N_BUCKETS - 1]], axis=1).astype(F32),
        c_w_q=[w.astype(BF16) for w in c_w_q], c_w_kv=[w.astype(BF16) for w in c_w_kv],
        c_w_out=[w.astype(BF16) for w in c_w_out],
        f_w1=[w.astype(BF16) for w in f_w1], f_w2=[w.astype(BF16) for w in f_w2],
    )


def _encoder(x, mem, p, bias, t):
    depth = len(p["g_mix"])
    for i in range(depth):
        if i % 2 == 0:
            qkv, o, gt = _proj_a(x, p["g_mix"][i], p["a_w_in"], p["a_w_gate_t"], p["a_b_gate"])
            hf, hb = _mlstm(qkv, gt)
            x = _post_a(hf, hb, o, x, p["a_norm_g"], p["a_w_out"])
        else:
            qt, k, vt = _proj_b(x, p["g_mix"][i], p["b_w_q_t"], p["b_w_k"], p["b_w_v_t"], t)
            o = _attn(qt, k, vt, bias, p["cfar"], p["b_lambda"], p["b_subln_g"], _lambda_init(i))
            x = _out_b(o, x, p["b_w_out"])
        kv = _mem_kv(mem, p["g_mem"][i], p["c_w_kv"][i])
        x = _cross(x, p["g_cross"][i], p["c_w_q"][i], kv, p["c_w_out"][i])
        x = _mlp(x, p["g_mlp"][i], p["f_w1"][i], p["f_w2"][i], p["g_final"], i == depth - 1)
    return x


def kernel(x_prompt, x_sample, mem_prompt, mem_sample, g_mix, g_cross, g_mem, g_mlp, g_final, a_w_in, a_w_gate, a_b_gate, a_norm_g, a_w_out, b_w_qkv, b_lambda, b_subln_g, b_w_out, rel_bias, c_w_q, c_w_kv, c_w_out, f_w1, f_w2):
    p = _prepare(g_mix, g_cross, g_mem, g_mlp, g_final, a_w_in, a_w_gate, a_b_gate, a_norm_g,
                 a_w_out, b_w_qkv, b_lambda, b_subln_g, b_w_out, rel_bias, c_w_q, c_w_kv,
                 c_w_out, f_w1, f_w2)
    t = min(TOKEN_TILE, x_prompt.shape[1], x_sample.shape[1])
    bias = _bias_tiles(p["rel_bias"], t)
    y_prompt = _encoder(x_prompt, mem_prompt, p, bias, t)
    y_sample = _encoder(x_sample, mem_sample, p, bias, t)
    return (y_prompt, y_sample)
```

```python
import functools
import math

import jax
import jax.numpy as jnp
from jax import lax
from jax.experimental import pallas as pl
from jax.experimental.pallas import tpu as pltpu

EPS = 1e-6
HA = 4
DK_A = 128
DV_A = 256
CHUNK = 128
HB = 8
DH_B = 64
N_BUCKETS = 32
MAX_DIST = 128
HC = 4

F32 = jnp.float32
BF16 = jnp.bfloat16

VMEM_LIMIT_BYTES = 56 * 1024 * 1024
TOKEN_TILE = 512
NT_DIMS = (((1,), (1,)), ((), ()))
LOG2E = math.log2(math.e)
V_ROWS = 2 * DH_B + 16
NEAR_STEPS = 3
FAR_UNROLL = 2
NORM_MARGIN = 1.01
FIXED_MAX_HEADROOM = 64.0


def _lambda_init(layer):
    return 0.8 - 0.6 * math.exp(-0.3 * layer)


def _params(*sem):
    return pltpu.CompilerParams(dimension_semantics=sem, vmem_limit_bytes=VMEM_LIMIT_BYTES)


def _resident(shape):
    zeros = (0,) * len(shape)
    return pl.BlockSpec(shape, lambda *_: zeros, pipeline_mode=pl.Buffered(1))


def _rms(x, g):
    return x * lax.rsqrt(jnp.mean(x * x, axis=-1, keepdims=True) + EPS) * g


def _dot(a, b):
    return jnp.dot(a, b, preferred_element_type=F32)


def _proj_a_kernel(x_ref, g_ref, w_ref, wgt_ref, bg_ref, qkv_ref, o_ref, gt_ref):
    xn = _rms(x_ref[...], g_ref[...]).astype(BF16)
    nqk = HA * DK_A
    for c in range(4):
        y = _dot(xn, w_ref[:, c * nqk:(c + 1) * nqk])
        if c == 1:
            y = y * (DK_A ** -0.5)
        qkv_ref[:, c * nqk:(c + 1) * nqk] = y.astype(BF16)
    for c in range(2):
        o_ref[:, c * nqk:(c + 1) * nqk] = _dot(xn, w_ref[:, (4 + c) * nqk:(5 + c) * nqk])
    gt = lax.dot_general(wgt_ref[...], xn, NT_DIMS, preferred_element_type=F32)
    gt_ref[...] = gt + bg_ref[...]


def _proj_a(x, g, w_in, w_gate_t, b_gate):
    B, S, D = x.shape
    ts = min(TOKEN_TILE, S)
    n_qkv = 2 * HA * DK_A + HA * DV_A
    n_o = HA * DV_A
    return pl.pallas_call(
        _proj_a_kernel,
        grid=(B, S // ts),
        in_specs=[
            pl.BlockSpec((None, ts, D), lambda b, i: (b, i, 0)),
            _resident((1, D)),
            _resident(w_in.shape),
            _resident(w_gate_t.shape),
            _resident(b_gate.shape),
        ],
        out_specs=[
            pl.BlockSpec((None, ts, n_qkv), lambda b, i: (b, i, 0)),
            pl.BlockSpec((None, ts, n_o), lambda b, i: (b, i, 0)),
            pl.BlockSpec((None, 4 * HA, ts), lambda b, i: (b, 0, i)),
        ],
        out_shape=[
            jax.ShapeDtypeStruct((B, S, n_qkv), BF16),
            jax.ShapeDtypeStruct((B, S, n_o), F32),
            jax.ShapeDtypeStruct((B, 4 * HA, S), F32),
        ],
        compiler_params=_params("parallel", "parallel"),
        name="proj_a",
    )(x, g, w_in, w_gate_t, b_gate)


def _log_sigmoid(x):
    return jnp.minimum(x, 0.0) - jnp.log1p(jnp.exp(-jnp.abs(x)))


def _mlstm_kernel(qf_ref, kf_ref, vf_ref, qb_ref, kb_ref, vb_ref, gf_ref, gb_ref,
                  hf_ref, hb_ref, c_ref, n_ref, m_ref, t_ref):
    L = CHUNK

    @pl.when(pl.program_id(1) == 0)
    def _():
        c_ref[...] = jnp.zeros(c_ref.shape, F32)
        n_ref[...] = jnp.zeros(n_ref.shape, F32)
        m_ref[...] = jnp.zeros(m_ref.shape, F32)
        t_ref[...] = jnp.zeros(t_ref.shape, F32)

    gf = gf_ref[...]
    gb = gb_ref[...]
    row8 = lax.broadcasted_iota(jnp.int32, (2 * HA, L), 0)
    lane8 = lax.broadcasted_iota(jnp.int32, (2 * HA, L), 1)
    is_f = row8 < HA
    ig8 = jnp.where(is_f, gf, pltpu.roll(gb, HA, 0))
    fg8 = jnp.where(is_f, pltpu.roll(gf, HA, 0), gb)
    lf8 = _log_sigmoid(fg8)

    b8 = lf8
    d = 1
    while d < L:
        fwd = jnp.where(lane8 >= d, pltpu.roll(b8, d, 1), 0.0)
        bwd = jnp.where(lane8 < L - d, pltpu.roll(b8, L - d, 1), 0.0)
        b8 = b8 + jnp.where(is_f, fwd, bwd)
        d *= 2

    bl8 = jnp.where(is_f[:, 0:1], b8[:, L - 1:L], b8[:, 0:1])
    m_old8 = m_ref[:, 0:1]
    a8 = ig8 - b8
    wlog8 = bl8 + a8
    m_new8 = jnp.maximum(bl8 + m_old8, jnp.max(wlog8, axis=1, keepdims=True))
    w8 = jnp.exp(wlog8 - m_new8)
    decay8 = jnp.exp(bl8 + m_old8 - m_new8)

    t_ref[0:2 * HA, :] = b8
    t_ref[2 * HA:4 * HA, :] = w8
    tt = t_ref[...].T

    ti = lax.broadcasted_iota(jnp.int32, (L, L), 0)
    si = lax.broadcasted_iota(jnp.int32, (L, L), 1)

    for r in range(2 * HA):
        h = r % HA
        if r < HA:
            q_ref, k_ref, v_ref, out_ref, mask = qf_ref, kf_ref, vf_ref, hf_ref, si <= ti
        else:
            q_ref, k_ref, v_ref, out_ref, mask = qb_ref, kb_ref, vb_ref, hb_ref, si >= ti
        q = q_ref[:, h * DK_A:(h + 1) * DK_A]
        k = k_ref[:, h * DK_A:(h + 1) * DK_A]
        v = v_ref[:, h * DV_A:(h + 1) * DV_A]
        b_col = tt[:, r:r + 1]
        w_col = tt[:, 2 * HA + r:2 * HA + r + 1]
        m_old = m_old8[r:r + 1, :]

        dlog = jnp.where(mask, b_col + a8[r:r + 1, :], -jnp.inf)
        inter = b_col + m_old
        m_t = jnp.maximum(inter, jnp.max(dlog, axis=1, keepdims=True))
        s = lax.dot_general(q, k, NT_DIMS, preferred_element_type=F32) * jnp.exp(dlog - m_t)
        iw = jnp.exp(inter - m_t)
        c_old = c_ref[r]
        n_old = n_ref[r:r + 1, :]
        num = iw * _dot(q, c_old.astype(BF16)) + _dot(s.astype(BF16), v)
        den = (iw * jnp.sum(q.astype(F32) * n_old, axis=1, keepdims=True)
               + jnp.sum(s, axis=1, keepdims=True))
        inv = 1.0 / jnp.maximum(jnp.abs(den), jnp.exp(-m_t))
        out_ref[:, h * DV_A:(h + 1) * DV_A] = num * inv

        wk = w_col * k.astype(F32)
        dec = decay8[r:r + 1, :]
        n_ref[r:r + 1, :] = dec * n_old + jnp.sum(wk, axis=0, keepdims=True)
        c_ref[r] = dec * c_old + _dot(wk.T.astype(BF16), v)

    m_ref[...] = jnp.broadcast_to(m_new8, m_ref.shape)


def _mlstm(qkv, gt):
    B, S, _ = qkv.shape
    L = CHUNK
    nc = S // L
    nqk = HA * DK_A
    nv = HA * DV_A

    def fwd(col):
        return lambda b, c: (b, c, col)

    def bwd(col):
        return lambda b, c: (b, nc - 1 - c, col)

    return pl.pallas_call(
        _mlstm_kernel,
        grid=(B, nc),
        in_specs=[
            pl.BlockSpec((None, L, nqk), fwd(0)),
            pl.BlockSpec((None, L, nqk), fwd(1)),
            pl.BlockSpec((None, L, nv), fwd(1)),
            pl.BlockSpec((None, L, nqk), bwd(0)),
            pl.BlockSpec((None, L, nqk), bwd(1)),
            pl.BlockSpec((None, L, nv), bwd(1)),
            pl.BlockSpec((None, 2 * HA, L), lambda b, c: (b, 0, c)),
            pl.BlockSpec((None, 2 * HA, L), lambda b, c: (b, 1, nc - 1 - c)),
        ],
        out_specs=[
            pl.BlockSpec((None, L, nv), fwd(0)),
            pl.BlockSpec((None, L, nv), bwd(0)),
        ],
        out_shape=[jax.ShapeDtypeStruct((B, S, nv), F32)] * 2,
        scratch_shapes=[
            pltpu.VMEM((2 * HA, DK_A, DV_A), F32),
            pltpu.VMEM((2 * HA, DK_A), F32),
            pltpu.VMEM((2 * HA, 128), F32),
            pltpu.VMEM((128, L), F32),
        ],
        compiler_params=_params("parallel", "arbitrary"),
        name="mlstm",
    )(qkv, qkv, qkv, qkv, qkv, qkv, gt, gt)


def _post_a_kernel(hf_ref, hb_ref, o_ref, x_ref, ng_ref, w_ref, out_ref):
    parts = []
    for h in range(HA):
        sl = slice(h * DV_A, (h + 1) * DV_A)
        hh = hf_ref[:, sl] + hb_ref[:, sl]
        hn = hh * lax.rsqrt(jnp.mean(hh * hh, axis=-1, keepdims=True) + EPS)
        hn = hn * ng_ref[:, sl] * jax.nn.sigmoid(o_ref[:, sl])
        parts.append(hn.astype(BF16))
    out_ref[...] = x_ref[...] + _dot(jnp.concatenate(parts, axis=1), w_ref[...])


def _post_a(hf, hb, o, x, norm_g, w_out):
    B, S, D = x.shape
    nv = hf.shape[-1]
    ts = min(TOKEN_TILE, S)
    tile = lambda n: pl.BlockSpec((None, ts, n), lambda b, i: (b, i, 0))
    return pl.pallas_call(
        _post_a_kernel,
        grid=(B, S // ts),
        in_specs=[tile(nv), tile(nv), tile(nv), tile(D), _resident((1, nv)), _resident(w_out.shape)],
        out_specs=tile(D),
        out_shape=jax.ShapeDtypeStruct((B, S, D), F32),
        compiler_params=_params("parallel", "parallel"),
        name="post_a",
    )(hf, hb, o, x, norm_g, w_out)


def _proj_b_kernel(x_ref, g_ref, wqt_ref, wk_ref, wvt_ref, qt_ref, k_ref, vt_ref):
    xn = _rms(x_ref[...], g_ref[...]).astype(BF16)
    qt = lax.dot_general(wqt_ref[...], xn, NT_DIMS, preferred_element_type=F32)
    qt_ref[...] = (qt * (DH_B ** -0.5 * LOG2E)).astype(BF16)
    vt = lax.dot_general(wvt_ref[...], xn, NT_DIMS, preferred_element_type=F32)
    dv = 2 * DH_B
    for h in range(HB):
        vt_ref[h, 0:dv, :] = vt[h * dv:(h + 1) * dv].astype(BF16)
        vt_ref[h, dv:, :] = jnp.ones((V_ROWS - dv, vt.shape[1]), BF16)
    k_ref[...] = _dot(xn, wk_ref[...]).astype(BF16)


def _proj_b(x, g, w_q_t, w_k, w_v_t, t):
    B, S, D = x.shape
    nb = S // t
    n = w_k.shape[1]
    return pl.pallas_call(
        _proj_b_kernel,
        grid=(B, nb),
        in_specs=[
            pl.BlockSpec((None, t, D), lambda b, i: (b, i, 0)),
            _resident((1, D)),
            _resident(w_q_t.shape),
            _resident(w_k.shape),
            _resident(w_v_t.shape),
        ],
        out_specs=[
            pl.BlockSpec((None, None, n, t), lambda b, i: (b, i, 0, 0)),
            pl.BlockSpec((None, t, n), lambda b, i: (b, i, 0)),
            pl.BlockSpec((None, None, HB, V_ROWS, t), lambda b, i: (b, i, 0, 0, 0)),
        ],
        out_shape=[
            jax.ShapeDtypeStruct((B, nb, n, t), BF16),
            jax.ShapeDtypeStruct((B, S, n), BF16),
            jax.ShapeDtypeStruct((B, nb, HB, V_ROWS, t), BF16),
        ],
        compiler_params=_params("parallel", "parallel"),
        name="proj_b",
    )(x, g, w_q_t, w_k, w_v_t)


def _bias_kernel(tab_ref, out_ref, *, t):
    h = pl.program_id(0)
    off = (pl.program_id(1) - 1) * t
    ki = lax.broadcasted_iota(jnp.int32, (t, t), 0)
    qi = lax.broadcasted_iota(jnp.int32, (t, t), 1)
    rp = off + ki - qi
    half = N_BUCKETS // 2
    max_exact = half // 2
    ret = (rp > 0).astype(jnp.int32) * half
    n = jnp.abs(rp)
    nf = jnp.maximum(n, 1).astype(F32)
    large = max_exact + (jnp.log(nf / max_exact) / math.log(MAX_DIST / max_exact)
                         * (half - max_exact)).astype(jnp.int32)
    large = jnp.minimum(large, half - 1)
    bucket = ret + jnp.where(n < max_exact, n, large)
    acc = jnp.zeros((t, t), F32)
    for b in range(N_BUCKETS):
        acc = jnp.where(bucket == b, tab_ref[b, h], acc)
    out_ref[...] = acc * LOG2E


def _bias_tiles(rel_bias, t):
    return pl.pallas_call(
        functools.partial(_bias_kernel, t=t),
        grid=(HB, 3),
        in_specs=[pl.BlockSpec(memory_space=pltpu.SMEM)],
        out_specs=pl.BlockSpec((None, None, t, t), lambda h, j: (h, j, 0, 0)),
        out_shape=jax.ShapeDtypeStruct((HB, 3, t, t), F32),
        compiler_params=_params("parallel", "parallel"),
        name="bias_tiles",
    )(rel_bias)


def _attn_kernel(qt_ref, k_ref, vt_ref, bias_ref, cfar_ref, lam_ref, g_ref, o_ref,
                 q1_ref, q2_ref, m_ref, acc_ref, s0_ref, s1_ref, p0_ref, p1_ref, mb_ref, al_ref,
                 kn_ref, *, t, nk, lambda_init):
    h = pl.program_id(1)
    qi = pl.program_id(2)
    s_refs = (s0_ref, s1_ref)
    p_refs = (p0_ref, p1_ref)
    dv = 2 * DH_B

    @pl.when(qi == 0)
    def _():
        first_half = lax.broadcasted_iota(jnp.int32, (t, dv), 1) < DH_B

        def body(i, carry):
            kf = k_ref[pl.ds(pl.multiple_of(i * t, t), t), :].astype(F32)
            sq = kf * kf
            n1 = jnp.sum(jnp.where(first_half, sq, 0.0), axis=1, keepdims=True)
            n2 = jnp.sum(jnp.where(first_half, 0.0, sq), axis=1, keepdims=True)
            return (jnp.maximum(carry[0], jnp.max(n1, axis=0, keepdims=True)),
                    jnp.maximum(carry[1], jnp.max(n2, axis=0, keepdims=True)))

        zero = jnp.zeros((1, 1), F32)
        n1, n2 = lax.fori_loop(0, nk, body, (zero, zero))
        kn_ref[0:1, :] = jnp.broadcast_to(n1, (1, kn_ref.shape[1]))
        kn_ref[1:2, :] = jnp.broadcast_to(n2, (1, kn_ref.shape[1]))

    qt = qt_ref[...]
    rows = lax.broadcasted_iota(jnp.int32, qt.shape, 0)
    q1_ref[...] = jnp.where(rows < DH_B, qt, jnp.zeros_like(qt))
    q2_ref[...] = jnp.where(rows >= DH_B, qt, jnp.zeros_like(qt))
    m_ref[...] = jnp.full(m_ref.shape, -jnp.inf, F32)
    acc_ref[...] = jnp.zeros(acc_ref.shape, F32)
    c_left = cfar_ref[h, 0] * LOG2E
    c_right = cfar_ref[h, 1] * LOG2E

    def is_near(kj):
        return jnp.abs(kj - qi) <= 1

    def block_of(n):
        x = qi - 1 + n
        return jnp.where(x < 0, x + nk, jnp.where(x >= nk, x - nk, x))

    def stage_a(n, par):
        kj = block_of(n)
        ks = pl.multiple_of(kj * t, t)
        kb = k_ref[pl.ds(ks, t), :]
        for j, q_ref in enumerate((q1_ref, q2_ref)):
            s = _dot(kb, q_ref[...])
            s_refs[par][j] = s
            mb_ref[par, j:j + 1, :] = jnp.max(s, axis=0, keepdims=True)

    def add_near_bias(n, par):
        if n >= NEAR_STEPS:
            return
        kj = block_of(n)

        @pl.when(is_near(kj))
        def _():
            bt = bias_ref[kj - qi + 1]
            for j in range(2):
                s = s_refs[par][j] + bt
                s_refs[par][j] = s
                mb_ref[par, j:j + 1, :] = jnp.max(s, axis=0, keepdims=True)

    def stage_b(n, par, maybe_near):
        kj = block_of(n)
        c = jnp.where(kj < qi, c_left, c_right)
        if maybe_near:
            c = jnp.where(is_near(kj), 0.0, c)
        for j in range(2):
            m_old = m_ref[j:j + 1, :]
            m_new = jnp.maximum(m_old, mb_ref[par, j:j + 1, :] + c)
            p_refs[par][j] = jnp.exp2(s_refs[par][j] - (m_new - c)).astype(BF16)
            al_ref[par, j:j + 1, :] = jnp.exp2(m_old - m_new)
            m_ref[j:j + 1, :] = m_new

    def stage_c(n, par):
        vt = vt_ref[block_of(n)]
        for j in range(2):
            acc_ref[j] = al_ref[par, j:j + 1, :] * acc_ref[j] + _dot(vt, p_refs[par][j])

    def step(n, par, maybe_near):
        stage_a(n + 2, par)
        stage_b(n + 1, 1 - par, maybe_near)
        stage_c(n, par)

    stage_a(0, 0)
    add_near_bias(0, 0)
    stage_a(1, 1)
    stage_b(0, 0, True)
    for n in range(NEAR_STEPS - 2):
        add_near_bias(n + 1, (n + 1) % 2)
        step(n, n % 2, True)
    n = NEAR_STEPS - 2
    add_near_bias(n + 1, (n + 1) % 2)
    stage_b(n + 1, (n + 1) % 2, True)
    stage_c(n, n % 2)
    stage_c(n + 1, (n + 1) % 2)

    cmax = jnp.maximum(c_left, c_right)
    worst = jnp.full((1, t), -jnp.inf, F32)
    for j, q_ref in enumerate((q1_ref, q2_ref)):
        qf = q_ref[...].astype(F32)
        qn2 = jnp.sum(qf * qf, axis=0, keepdims=True)
        bound = jnp.sqrt(qn2 * kn_ref[j:j + 1, 0:1]) * NORM_MARGIN + 1.0
        worst = jnp.maximum(worst, bound + cmax - m_ref[j:j + 1, :])
    safe = jnp.max(worst) <= FIXED_MAX_HEADROOM

    n_far = nk - NEAR_STEPS

    def fixed_ab(n, par):
        kj = block_of(n)
        ks = pl.multiple_of(kj * t, t)
        kb = k_ref[pl.ds(ks, t), :]
        c = jnp.where(kj < qi, c_left, c_right)
        for j, q_ref in enumerate((q1_ref, q2_ref)):
            s = _dot(kb, q_ref[...])
            p_refs[par][j] = jnp.exp2(s - (m_ref[j:j + 1, :] - c)).astype(BF16)

    def fixed_c(n, par):
        vt = vt_ref[block_of(n)]
        for j in range(2):
            acc_ref[j] = acc_ref[j] + _dot(vt, p_refs[par][j])

    @pl.when(safe)
    def _():
        def steps(n0, count):
            for u in range(count):
                par = (NEAR_STEPS + u) % 2
                fixed_ab(n0 + u + 1, 1 - par)
                fixed_c(n0 + u, par)

        def body(i, carry):
            steps(NEAR_STEPS + FAR_UNROLL * i, FAR_UNROLL)
            return carry

        fixed_ab(NEAR_STEPS, NEAR_STEPS % 2)
        lax.fori_loop(0, (n_far - 1) // FAR_UNROLL, body, 0)
        done = (n_far - 1) - (n_far - 1) % FAR_UNROLL
        steps(NEAR_STEPS + done, (n_far - 1) % FAR_UNROLL)
        fixed_c(nk - 1, (nk - 1) % 2)

    @pl.when(jnp.logical_not(safe))
    def _():
        def body(n, carry):
            stage_a(n, 0)
            stage_b(n, 0, False)
            stage_c(n, 0)
            return carry

        lax.fori_loop(NEAR_STEPS, nk, body, 0)

    lv = lam_ref[...]
    lam = (jnp.exp(jnp.sum(lv[0:1] * lv[1:2], axis=1, keepdims=True))
           - jnp.exp(jnp.sum(lv[2:3] * lv[3:4], axis=1, keepdims=True)) + lambda_init)
    o = (acc_ref[0, 0:dv, :] * (1.0 / acc_ref[0, dv:dv + 1, :])
         - lam * (acc_ref[1, 0:dv, :] * (1.0 / acc_ref[1, dv:dv + 1, :])))
    on = o * lax.rsqrt(jnp.mean(o * o, axis=0, keepdims=True) + EPS) * g_ref[...]
    on = on * (1.0 - lambda_init)
    o_ref[...] = on.T.astype(BF16)


def _attn(qt, k, vt, bias, cfar, lam_vecs, subln_g, lambda_init):
    B, nb, n, t = qt.shape
    S = k.shape[1]
    assert nb % 2 == 0 and nb >= 4, nb
    dh2 = 2 * DH_B
    return pl.pallas_call(
        functools.partial(_attn_kernel, t=t, nk=nb, lambda_init=lambda_init),
        grid=(B, HB, nb),
        in_specs=[
            pl.BlockSpec((None, None, dh2, t), lambda b, h, i: (b, i, h, 0)),
            pl.BlockSpec((None, S, dh2), lambda b, h, i: (b, 0, h)),
            pl.BlockSpec((None, nb, None, V_ROWS, t), lambda b, h, i: (b, 0, h, 0, 0)),
            pl.BlockSpec((None, 3, t, t), lambda b, h, i: (h, 0, 0, 0)),
            pl.BlockSpec(memory_space=pltpu.SMEM),
            _resident(lam_vecs.shape),
            _resident(subln_g.shape),
        ],
        out_specs=pl.BlockSpec((None, t, dh2), lambda b, h, i: (b, i, h)),
        out_shape=jax.ShapeDtypeStruct((B, S, n), BF16),
        scratch_shapes=[
            pltpu.VMEM((dh2, t), BF16),
            pltpu.VMEM((dh2, t), BF16),
            pltpu.VMEM((2, t), F32),
            pltpu.VMEM((2, V_ROWS, t), F32),
            pltpu.VMEM((2, t, t), F32),
            pltpu.VMEM((2, t, t), F32),
            pltpu.VMEM((2, t, t), BF16),
            pltpu.VMEM((2, t, t), BF16),
            pltpu.VMEM((2, 2, t), F32),
            pltpu.VMEM((2, 2, t), F32),
            pltpu.VMEM((2, 128), F32),
        ],
        compiler_params=_params("parallel", "parallel", "arbitrary"),
        name="diff_attn",
    )(qt, k, vt, bias, cfar, lam_vecs, subln_g)


def _out_b_kernel(o_ref, x_ref, w_ref, out_ref):
    out_ref[...] = x_ref[...] + _dot(o_ref[...], w_ref[...])


def _out_b(o, x, w_out):
    B, S, D = x.shape
    ts = min(TOKEN_TILE, S)
    tile = lambda n: pl.BlockSpec((None, ts, n), lambda b, i: (b, i, 0))
    return pl.pallas_call(
        _out_b_kernel,
        grid=(B, S // ts),
        in_specs=[tile(o.shape[-1]), tile(D), _resident(w_out.shape)],
        out_specs=tile(D),
        out_shape=jax.ShapeDtypeStruct((B, S, D), F32),
        compiler_params=_params("parallel", "parallel"),
        name="out_b",
    )(o, x, w_out)


def _mem_kv_kernel(mem_ref, g_ref, w_ref, kv_ref):
    mn = _rms(mem_ref[...], g_ref[...]).astype(BF16)
    kv_ref[...] = _dot(mn, w_ref[...]).astype(BF16)


def _mem_kv(mem, g, w_kv):
    B, M, D = mem.shape
    n = w_kv.shape[1]
    return pl.pallas_call(
        _mem_kv_kernel,
        grid=(B,),
        in_specs=[pl.BlockSpec((None, M, D), lambda b: (b, 0, 0)), _resident((1, D)),
                  _resident(w_kv.shape)],
        out_specs=pl.BlockSpec((None, M, n), lambda b: (b, 0, 0)),
        out_shape=jax.ShapeDtypeStruct((B, M, n), BF16),
        compiler_params=_params("parallel"),
        name="mem_kv",
    )(mem, g, w_kv)


def _cross_kernel(x_ref, g_ref, wq_ref, kv_ref, wo_ref, out_ref):
    x = x_ref[...]
    D = x.shape[-1]
    dc = D // HC
    xn = _rms(x, g_ref[...]).astype(BF16)
    q = _dot(xn, wq_ref[...])
    parts = []
    for h in range(HC):
        qh = (q[:, h * dc:(h + 1) * dc] * (dc ** -0.5)).astype(BF16)
        kh = kv_ref[:, h * dc:(h + 1) * dc]
        vh = kv_ref[:, D + h * dc:D + (h + 1) * dc]
        s = lax.dot_general(qh, kh, NT_DIMS, preferred_element_type=F32)
        p = jnp.exp(s - jnp.max(s, axis=-1, keepdims=True))
        inv = 1.0 / jnp.sum(p, axis=-1, keepdims=True)
        parts.append((_dot(p.astype(BF16), vh) * inv).astype(BF16))
    out_ref[...] = x + _dot(jnp.concatenate(parts, axis=1), wo_ref[...])


def _cross(x, g, w_q, kv, w_out):
    B, S, D = x.shape
    M = kv.shape[1]
    ts = min(TOKEN_TILE, S)
    tile = pl.BlockSpec((None, ts, D), lambda b, i: (b, i, 0))
    return pl.pallas_call(
        _cross_kernel,
        grid=(B, S // ts),
        in_specs=[tile, _resident((1, D)), _resident(w_q.shape),
                  pl.BlockSpec((None, M, 2 * D), lambda b, i: (b, 0, 0)),
                  _resident(w_out.shape)],
        out_specs=tile,
        out_shape=jax.ShapeDtypeStruct((B, S, D), F32),
        compiler_params=_params("parallel", "parallel"),
        name="cross_attn",
    )(x, g, w_q, kv, w_out)


def _mlp_kernel(x_ref, g_ref, w1_ref, w2_ref, gf_ref, out_ref, *, final_norm):
    x = x_ref[...]
    D = x.shape[-1]
    xn = _rms(x, g_ref[...]).astype(BF16)
    acc = x
    for c in range(w1_ref.shape[1] // D):
        hcol = _dot(xn, w1_ref[:, c * D:(c + 1) * D])
        hcol = jnp.square(jnp.maximum(hcol, 0.0)).astype(BF16)
        acc = acc + _dot(hcol, w2_ref[c * D:(c + 1) * D, :])
    out_ref[...] = _rms(acc, gf_ref[...]) if final_norm else acc


def _mlp(x, g, w1, w2, g_final, final_norm):
    B, S, D = x.shape
    ts = min(TOKEN_TILE, S)
    tile = pl.BlockSpec((None, ts, D), lambda b, i: (b, i, 0))
    return pl.pallas_call(
        functools.partial(_mlp_kernel, final_norm=final_norm),
        grid=(B, S // ts),
        in_specs=[tile, _resident((1, D)), _resident(w1.shape), _resident(w2.shape),
                  _resident((1, D))],
        out_specs=tile,
        out_shape=jax.ShapeDtypeStruct((B, S, D), F32),
        compiler_params=_params("parallel", "parallel"),
        name="mlp",
    )(x, g, w1, w2, g_final)


def _prepare(g_mix, g_cross, g_mem, g_mlp, g_final, a_w_in, a_w_gate, a_b_gate, a_norm_g,
             a_w_out, b_w_qkv, b_lambda, b_subln_g, b_w_out, rel_bias, c_w_q, c_w_kv,
             c_w_out, f_w1, f_w2):
    row = lambda v: v.reshape(1, -1).astype(F32)
    nq = 2 * HB * DH_B
    wqkv = b_w_qkv[0]
    return dict(
        g_mix=[row(g) for g in g_mix], g_cross=[row(g) for g in g_cross],
        g_mem=[row(g) for g in g_mem], g_mlp=[row(g) for g in g_mlp], g_final=row(g_final),
        a_w_in=a_w_in[0].astype(BF16), a_w_gate_t=a_w_gate[0].T.astype(BF16),
        a_b_gate=a_b_gate[0].reshape(-1, 1).astype(F32), a_norm_g=row(a_norm_g[0]),
        a_w_out=a_w_out[0].astype(BF16),
        b_w_q_t=wqkv[:, :nq].T.astype(BF16), b_w_k=wqkv[:, nq:2 * nq].astype(BF16),
        b_w_v_t=wqkv[:, 2 * nq:].T.astype(BF16),
        b_lambda=b_lambda[0].astype(F32), b_subln_g=b_subln_g[0].reshape(-1, 1).astype(F32),
        b_w_out=b_w_out[0].astype(BF16),
        rel_bias=rel_bias.astype(F32),
        cfar=jnp.stack([rel_bias[N_BUCKETS // 2 - 1], rel_bias[N_BUCKETS - 1]], axis=1).astype(F32),
        c_w_q=[w.astype(BF16) for w in c_w_q], c_w_kv=[w.astype(BF16) for w in c_w_kv],
        c_w_out=[w.astype(BF16) for w in c_w_out],
        f_w1=[w.astype(BF16) for w in f_w1], f_w2=[w.astype(BF16) for w in f_w2],
    )


def _encoder(x, mem, p, bias, t):
    depth = len(p["g_mix"])
    for i in range(depth):
        if i % 2 == 0:
            qkv, o, gt = _proj_a(x, p["g_mix"][i], p["a_w_in"], p["a_w_gate_t"], p["a_b_gate"])
            hf, hb = _mlstm(qkv, gt)
            x = _post_a(hf, hb, o, x, p["a_norm_g"], p["a_w_out"])
        else:
            qt, k, vt = _proj_b(x, p["g_mix"][i], p["b_w_q_t"], p["b_w_k"], p["b_w_v_t"], t)
            o = _attn(qt, k, vt, bias, p["cfar"], p["b_lambda"], p["b_subln_g"], _lambda_init(i))
            x = _out_b(o, x, p["b_w_out"])
        kv = _mem_kv(mem, p["g_mem"][i], p["c_w_kv"][i])
        x = _cross(x, p["g_cross"][i], p["c_w_q"][i], kv, p["c_w_out"][i])
        x = _mlp(x, p["g_mlp"][i], p["f_w1"][i], p["f_w2"][i], p["g_final"], i == depth - 1)
    return x


def kernel(x_prompt, x_sample, mem_prompt, mem_sample, g_mix, g_cross, g_mem, g_mlp, g_final, a_w_in, a_w_gate, a_b_gate, a_norm_g, a_w_out, b_w_qkv, b_lambda, b_subln_g, b_w_out, rel_bias, c_w_q, c_w_kv, c_w_out, f_w1, f_w2):
    p = _prepare(g_mix, g_cross, g_mem, g_mlp, g_final, a_w_in, a_w_gate, a_b_gate, a_norm_g,
                 a_w_out, b_w_qkv, b_lambda, b_subln_g, b_w_out, rel_bias, c_w_q, c_w_kv,
                 c_w_out, f_w1, f_w2)
    t = min(TOKEN_TILE, x_prompt.shape[1], x_sample.shape[1])
    bias = _bias_tiles(p["rel_bias"], t)
    y_prompt = _encoder(x_prompt, mem_prompt, p, bias, t)
    y_sample = _encoder(x_sample, mem_sample, p, bias, t)
    return (y_prompt, y_sample)
```

```python
import functools
import math

import jax
import jax.numpy as jnp
from jax import lax
from jax.experimental import pallas as pl
from jax.experimental.pallas import tpu as pltpu

EPS = 1e-6
HA = 4
DK_A = 128
DV_A = 256
CHUNK = 128
HB = 8
DH_B = 64
N_BUCKETS = 32
MAX_DIST = 128
HC = 4

F32 = jnp.float32
BF16 = jnp.bfloat16

VMEM_LIMIT_BYTES = 56 * 1024 * 1024
TOKEN_TILE = 512
NT_DIMS = (((1,), (1,)), ((), ()))
LOG2E = math.log2(math.e)
NEAR_STEPS = 3
FAR_UNROLL = 4
NORM_MARGIN = 1.01
BOUND_SHIFT_MAX_SPREAD = 100.0


def _lambda_init(layer):
    return 0.8 - 0.6 * math.exp(-0.3 * layer)


def _params(*sem):
    return pltpu.CompilerParams(dimension_semantics=sem, vmem_limit_bytes=VMEM_LIMIT_BYTES)


def _resident(shape):
    zeros = (0,) * len(shape)
    return pl.BlockSpec(shape, lambda *_: zeros, pipeline_mode=pl.Buffered(1))


def _rms(x, g):
    return x * lax.rsqrt(jnp.mean(x * x, axis=-1, keepdims=True) + EPS) * g


def _dot(a, b):
    return jnp.dot(a, b, preferred_element_type=F32)


def _proj_a_kernel(x_ref, g_ref, w_ref, wgt_ref, bg_ref, qkv_ref, o_ref, gt_ref):
    xn = _rms(x_ref[...], g_ref[...]).astype(BF16)
    nqk = HA * DK_A
    for c in range(4):
        y = _dot(xn, w_ref[:, c * nqk:(c + 1) * nqk])
        if c == 1:
            y = y * (DK_A ** -0.5)
        qkv_ref[:, c * nqk:(c + 1) * nqk] = y.astype(BF16)
    for c in range(2):
        o_ref[:, c * nqk:(c + 1) * nqk] = _dot(xn, w_ref[:, (4 + c) * nqk:(5 + c) * nqk])
    gt = lax.dot_general(wgt_ref[...], xn, NT_DIMS, preferred_element_type=F32)
    gt_ref[...] = gt + bg_ref[...]


def _proj_a(x, g, w_in, w_gate_t, b_gate):
    B, S, D = x.shape
    ts = min(TOKEN_TILE, S)
    n_qkv = 2 * HA * DK_A + HA * DV_A
    n_o = HA * DV_A
    return pl.pallas_call(
        _proj_a_kernel,
        grid=(B, S // ts),
        in_specs=[
            pl.BlockSpec((None, ts, D), lambda b, i: (b, i, 0)),
            _resident((1, D)),
            _resident(w_in.shape),
            _resident(w_gate_t.shape),
            _resident(b_gate.shape),
        ],
        out_specs=[
            pl.BlockSpec((None, ts, n_qkv), lambda b, i: (b, i, 0)),
            pl.BlockSpec((None, ts, n_o), lambda b, i: (b, i, 0)),
            pl.BlockSpec((None, 4 * HA, ts), lambda b, i: (b, 0, i)),
        ],
        out_shape=[
            jax.ShapeDtypeStruct((B, S, n_qkv), BF16),
            jax.ShapeDtypeStruct((B, S, n_o), F32),
            jax.ShapeDtypeStruct((B, 4 * HA, S), F32),
        ],
        compiler_params=_params("parallel", "parallel"),
        name="proj_a",
    )(x, g, w_in, w_gate_t, b_gate)


def _log_sigmoid(x):
    return jnp.minimum(x, 0.0) - jnp.log1p(jnp.exp(-jnp.abs(x)))


def _mlstm_kernel(qf_ref, kf_ref, vf_ref, qb_ref, kb_ref, vb_ref, gf_ref, gb_ref,
                  hf_ref, hb_ref, c_ref, n_ref, m_ref, t_ref):
    L = CHUNK

    @pl.when(pl.program_id(1) == 0)
    def _():
        c_ref[...] = jnp.zeros(c_ref.shape, F32)
        n_ref[...] = jnp.zeros(n_ref.shape, F32)
        m_ref[...] = jnp.zeros(m_ref.shape, F32)
        t_ref[...] = jnp.zeros(t_ref.shape, F32)

    gf = gf_ref[...]
    gb = gb_ref[...]
    row8 = lax.broadcasted_iota(jnp.int32, (2 * HA, L), 0)
    lane8 = lax.broadcasted_iota(jnp.int32, (2 * HA, L), 1)
    is_f = row8 < HA
    ig8 = jnp.where(is_f, gf, pltpu.roll(gb, HA, 0))
    fg8 = jnp.where(is_f, pltpu.roll(gf, HA, 0), gb)
    lf8 = _log_sigmoid(fg8)

    b8 = lf8
    d = 1
    while d < L:
        fwd = jnp.where(lane8 >= d, pltpu.roll(b8, d, 1), 0.0)
        bwd = jnp.where(lane8 < L - d, pltpu.roll(b8, L - d, 1), 0.0)
        b8 = b8 + jnp.where(is_f, fwd, bwd)
        d *= 2

    bl8 = jnp.where(is_f[:, 0:1], b8[:, L - 1:L], b8[:, 0:1])
    m_old8 = m_ref[:, 0:1]
    a8 = ig8 - b8
    wlog8 = bl8 + a8
    m_new8 = jnp.maximum(bl8 + m_old8, jnp.max(wlog8, axis=1, keepdims=True))
    w8 = jnp.exp(wlog8 - m_new8)
    decay8 = jnp.exp(bl8 + m_old8 - m_new8)

    t_ref[0:2 * HA, :] = b8
    t_ref[2 * HA:4 * HA, :] = w8
    tt = t_ref[...].T

    ti = lax.broadcasted_iota(jnp.int32, (L, L), 0)
    si = lax.broadcasted_iota(jnp.int32, (L, L), 1)

    for r in range(2 * HA):
        h = r % HA
        if r < HA:
            q_ref, k_ref, v_ref, out_ref, mask = qf_ref, kf_ref, vf_ref, hf_ref, si <= ti
        else:
            q_ref, k_ref, v_ref, out_ref, mask = qb_ref, kb_ref, vb_ref, hb_ref, si >= ti
        q = q_ref[:, h * DK_A:(h + 1) * DK_A]
        k = k_ref[:, h * DK_A:(h + 1) * DK_A]
        v = v_ref[:, h * DV_A:(h + 1) * DV_A]
        b_col = tt[:, r:r + 1]
        w_col = tt[:, 2 * HA + r:2 * HA + r + 1]
        m_old = m_old8[r:r + 1, :]

        dlog = jnp.where(mask, b_col + a8[r:r + 1, :], -jnp.inf)
        inter = b_col + m_old
        m_t = jnp.maximum(inter, jnp.max(dlog, axis=1, keepdims=True))
        s = lax.dot_general(q, k, NT_DIMS, preferred_element_type=F32) * jnp.exp(dlog - m_t)
        iw = jnp.exp(inter - m_t)
        c_old = c_ref[r]
        n_old = n_ref[r:r + 1, :]
        num = iw * _dot(q, c_old.astype(BF16)) + _dot(s.astype(BF16), v)
        den = (iw * jnp.sum(q.astype(F32) * n_old, axis=1, keepdims=True)
               + jnp.sum(s, axis=1, keepdims=True))
        inv = 1.0 / jnp.maximum(jnp.abs(den), jnp.exp(-m_t))
        out_ref[:, h * DV_A:(h + 1) * DV_A] = num * inv

        wk = w_col * k.astype(F32)
        dec = decay8[r:r + 1, :]
        n_ref[r:r + 1, :] = dec * n_old + jnp.sum(wk, axis=0, keepdims=True)
        c_ref[r] = dec * c_old + _dot(wk.T.astype(BF16), v)

    m_ref[...] = jnp.broadcast_to(m_new8, m_ref.shape)


def _mlstm(qkv, gt):
    B, S, _ = qkv.shape
    L = CHUNK
    nc = S // L
    nqk = HA * DK_A
    nv = HA * DV_A

    def fwd(col):
        return lambda b, c: (b, c, col)

    def bwd(col):
        return lambda b, c: (b, nc - 1 - c, col)

    return pl.pallas_call(
        _mlstm_kernel,
        grid=(B, nc),
        in_specs=[
            pl.BlockSpec((None, L, nqk), fwd(0)),
            pl.BlockSpec((None, L, nqk), fwd(1)),
            pl.BlockSpec((None, L, nv), fwd(1)),
            pl.BlockSpec((None, L, nqk), bwd(0)),
            pl.BlockSpec((None, L, nqk), bwd(1)),
            pl.BlockSpec((None, L, nv), bwd(1)),
            pl.BlockSpec((None, 2 * HA, L), lambda b, c: (b, 0, c)),
            pl.BlockSpec((None, 2 * HA, L), lambda b, c: (b, 1, nc - 1 - c)),
        ],
        out_specs=[
            pl.BlockSpec((None, L, nv), fwd(0)),
            pl.BlockSpec((None, L, nv), bwd(0)),
        ],
        out_shape=[jax.ShapeDtypeStruct((B, S, nv), F32)] * 2,
        scratch_shapes=[
            pltpu.VMEM((2 * HA, DK_A, DV_A), F32),
            pltpu.VMEM((2 * HA, DK_A), F32),
            pltpu.VMEM((2 * HA, 128), F32),
            pltpu.VMEM((128, L), F32),
        ],
        compiler_params=_params("parallel", "arbitrary"),
        name="mlstm",
    )(qkv, qkv, qkv, qkv, qkv, qkv, gt, gt)


def _post_a_kernel(hf_ref, hb_ref, o_ref, x_ref, ng_ref, w_ref, out_ref):
    parts = []
    for h in range(HA):
        sl = slice(h * DV_A, (h + 1) * DV_A)
        hh = hf_ref[:, sl] + hb_ref[:, sl]
        hn = hh * lax.rsqrt(jnp.mean(hh * hh, axis=-1, keepdims=True) + EPS)
        hn = hn * ng_ref[:, sl] * jax.nn.sigmoid(o_ref[:, sl])
        parts.append(hn.astype(BF16))
    out_ref[...] = x_ref[...] + _dot(jnp.concatenate(parts, axis=1), w_ref[...])


def _post_a(hf, hb, o, x, norm_g, w_out):
    B, S, D = x.shape
    nv = hf.shape[-1]
    ts = min(TOKEN_TILE, S)
    tile = lambda n: pl.BlockSpec((None, ts, n), lambda b, i: (b, i, 0))
    return pl.pallas_call(
        _post_a_kernel,
        grid=(B, S // ts),
        in_specs=[tile(nv), tile(nv), tile(nv), tile(D), _resident((1, nv)), _resident(w_out.shape)],
        out_specs=tile(D),
        out_shape=jax.ShapeDtypeStruct((B, S, D), F32),
        compiler_params=_params("parallel", "parallel"),
        name="post_a",
    )(hf, hb, o, x, norm_g, w_out)


def _proj_b_kernel(x_ref, g_ref, wqt_ref, wk_ref, wvt_ref, qt_ref, k_ref, vt_ref):
    xn = _rms(x_ref[...], g_ref[...]).astype(BF16)
    qt = lax.dot_general(wqt_ref[...], xn, NT_DIMS, preferred_element_type=F32)
    qt_ref[...] = (qt * (DH_B ** -0.5 * LOG2E)).astype(BF16)
    vt = lax.dot_general(wvt_ref[...], xn, NT_DIMS, preferred_element_type=F32)
    vt_ref[...] = vt.astype(BF16)
    k_ref[...] = _dot(xn, wk_ref[...]).astype(BF16)


def _proj_b(x, g, w_q_t, w_k, w_v_t, t):
    B, S, D = x.shape
    nb = S // t
    n = w_k.shape[1]
    return pl.pallas_call(
        _proj_b_kernel,
        grid=(B, nb),
        in_specs=[
            pl.BlockSpec((None, t, D), lambda b, i: (b, i, 0)),
            _resident((1, D)),
            _resident(w_q_t.shape),
            _resident(w_k.shape),
            _resident(w_v_t.shape),
        ],
        out_specs=[
            pl.BlockSpec((None, None, n, t), lambda b, i: (b, i, 0, 0)),
            pl.BlockSpec((None, t, n), lambda b, i: (b, i, 0)),
            pl.BlockSpec((None, None, n, t), lambda b, i: (b, i, 0, 0)),
        ],
        out_shape=[
            jax.ShapeDtypeStruct((B, nb, n, t), BF16),
            jax.ShapeDtypeStruct((B, S, n), BF16),
            jax.ShapeDtypeStruct((B, nb, n, t), BF16),
        ],
        compiler_params=_params("parallel", "parallel"),
        name="proj_b",
    )(x, g, w_q_t, w_k, w_v_t)


def _bias_kernel(tab_ref, out_ref, *, t):
    h = pl.program_id(0)
    off = (pl.program_id(1) - 1) * t
    ki = lax.broadcasted_iota(jnp.int32, (t, t), 0)
    qi = lax.broadcasted_iota(jnp.int32, (t, t), 1)
    rp = off + ki - qi
    half = N_BUCKETS // 2
    max_exact = half // 2
    ret = (rp > 0).astype(jnp.int32) * half
    n = jnp.abs(rp)
    nf = jnp.maximum(n, 1).astype(F32)
    large = max_exact + (jnp.log(nf / max_exact) / math.log(MAX_DIST / max_exact)
                         * (half - max_exact)).astype(jnp.int32)
    large = jnp.minimum(large, half - 1)
    bucket = ret + jnp.where(n < max_exact, n, large)
    acc = jnp.zeros((t, t), F32)
    for b in range(N_BUCKETS):
        acc = jnp.where(bucket == b, tab_ref[b, h], acc)
    out_ref[...] = acc * LOG2E


def _bias_tiles(rel_bias, t):
    return pl.pallas_call(
        functools.partial(_bias_kernel, t=t),
        grid=(HB, 3),
        in_specs=[pl.BlockSpec(memory_space=pltpu.SMEM)],
        out_specs=pl.BlockSpec((None, None, t, t), lambda h, j: (h, j, 0, 0)),
        out_shape=jax.ShapeDtypeStruct((HB, 3, t, t), F32),
        compiler_params=_params("parallel", "parallel"),
        name="bias_tiles",
    )(rel_bias)


def _attn_kernel(qt_ref, k_ref, vt_ref, bias_ref, tab_ref, lam_ref, g_ref, o_ref,
                 q1_ref, q2_ref, m_ref, acc_ref, l8_ref, s_ref, p0_ref, p1_ref, kn_ref,
                 *, t, nk, lambda_init):
    h = pl.program_id(1)
    qi = pl.program_id(2)
    p_refs = (p0_ref, p1_ref)
    dv = 2 * DH_B

    @pl.when(qi == 0)
    def _():
        first_half = lax.broadcasted_iota(jnp.int32, (t, dv), 1) < DH_B

        def body(i, carry):
            kf = k_ref[pl.ds(pl.multiple_of(i * t, t), t), :].astype(F32)
            sq = kf * kf
            n1 = jnp.sum(jnp.where(first_half, sq, 0.0), axis=1, keepdims=True)
            n2 = jnp.sum(jnp.where(first_half, 0.0, sq), axis=1, keepdims=True)
            return (jnp.maximum(carry[0], jnp.max(n1, axis=0, keepdims=True)),
                    jnp.maximum(carry[1], jnp.max(n2, axis=0, keepdims=True)))

        zero = jnp.zeros((1, 1), F32)
        n1, n2 = lax.fori_loop(0, nk, body, (zero, zero))
        kn_ref[0:1, :] = jnp.broadcast_to(n1, (1, kn_ref.shape[1]))
        kn_ref[1:2, :] = jnp.broadcast_to(n2, (1, kn_ref.shape[1]))

    qt = qt_ref[...]
    rows = lax.broadcasted_iota(jnp.int32, qt.shape, 0)
    q1_ref[...] = jnp.where(rows < DH_B, qt, jnp.zeros_like(qt))
    q2_ref[...] = jnp.where(rows >= DH_B, qt, jnp.zeros_like(qt))
    acc_ref[...] = jnp.zeros(acc_ref.shape, F32)
    l8_ref[...] = jnp.zeros(l8_ref.shape, F32)

    c_left = tab_ref[N_BUCKETS // 2 - 1, h] * LOG2E
    c_right = tab_ref[N_BUCKETS - 1, h] * LOG2E
    b_max = tab_ref[0, h]
    b_min = tab_ref[0, h]
    for b in range(1, N_BUCKETS):
        b_max = jnp.maximum(b_max, tab_ref[b, h])
        b_min = jnp.minimum(b_min, tab_ref[b, h])
    b_max = b_max * LOG2E
    b_min = b_min * LOG2E

    def is_near(kj):
        return jnp.abs(kj - qi) <= 1

    def block_of(n):
        x = qi - 1 + n
        return jnp.where(x < 0, x + nk, jnp.where(x >= nk, x - nk, x))

    def key_block(kj):
        return k_ref[pl.ds(pl.multiple_of(kj * t, t), t), :]

    spread = jnp.zeros((1, t), F32)
    for j, q_ref in enumerate((q1_ref, q2_ref)):
        qf = q_ref[...].astype(F32)
        qn2 = jnp.sum(qf * qf, axis=0, keepdims=True)
        bound = jnp.sqrt(qn2 * kn_ref[j:j + 1, 0:1]) * NORM_MARGIN + 1.0
        m_ref[j:j + 1, :] = bound + b_max
        spread = jnp.maximum(spread, 2.0 * bound + (b_max - b_min))
    safe = jnp.max(spread) <= BOUND_SHIFT_MAX_SPREAD

    def bound_ab(n, par, maybe_near):
        kj = block_of(n)
        kb = key_block(kj)
        c = jnp.where(kj < qi, c_left, c_right)
        if maybe_near:
            near = is_near(kj)
            c = jnp.where(near, 0.0, c)
            w_bias = jnp.where(near, 1.0, 0.0)
            bt = bias_ref[jnp.clip(kj - qi + 1, 0, NEAR_STEPS - 1)] * w_bias
        for j, q_ref in enumerate((q1_ref, q2_ref)):
            s = _dot(kb, q_ref[...])
            if maybe_near:
                s = s + bt
            p = jnp.exp2(s - (m_ref[j:j + 1, :] - c))
            l8_ref[j] = l8_ref[j] + p.reshape(t // 8, 8, t).sum(axis=0)
            p_refs[par][j] = p.astype(BF16)

    def bound_c(n, par):
        vt = vt_ref[block_of(n)]
        for j in range(2):
            acc_ref[j] = acc_ref[j] + _dot(vt, p_refs[par][j])

    @pl.when(safe)
    def _():
        def steps(n0, count, maybe_near):
            for u in range(count):
                par = (n0 + u) % 2 if isinstance(n0, int) else (NEAR_STEPS + u) % 2
                bound_ab(n0 + u + 1, 1 - par, maybe_near)
                bound_c(n0 + u, par)

        def body(i, carry):
            steps(NEAR_STEPS + FAR_UNROLL * i, FAR_UNROLL, False)
            return carry

        bound_ab(0, 0, True)
        steps(0, NEAR_STEPS - 1, True)
        steps(NEAR_STEPS - 1, 1, False)
        n_far = nk - 1 - NEAR_STEPS
        lax.fori_loop(0, n_far // FAR_UNROLL, body, 0)
        done = n_far - n_far % FAR_UNROLL
        steps(NEAR_STEPS + done, n_far % FAR_UNROLL, False)
        bound_c(nk - 1, (nk - 1) % 2)

    @pl.when(jnp.logical_not(safe))
    def _():
        m_ref[...] = jnp.full(m_ref.shape, -jnp.inf, F32)

        def body(n, carry):
            kj = block_of(n)
            kb = key_block(kj)
            near = is_near(kj)
            c = jnp.where(near, 0.0, jnp.where(kj < qi, c_left, c_right))
            for j, q_ref in enumerate((q1_ref, q2_ref)):
                s_ref[...] = _dot(kb, q_ref[...])

                @pl.when(near)
                def _():
                    s_ref[...] = s_ref[...] + bias_ref[jnp.clip(kj - qi + 1, 0, NEAR_STEPS - 1)]

                s = s_ref[...]
                m_old = m_ref[j:j + 1, :]
                m_new = jnp.maximum(m_old, jnp.max(s, axis=0, keepdims=True) + c)
                p = jnp.exp2(s - (m_new - c))
                alpha = jnp.exp2(m_old - m_new)
                l8_ref[j] = alpha * l8_ref[j] + p.reshape(t // 8, 8, t).sum(axis=0)
                acc_ref[j] = alpha * acc_ref[j] + _dot(vt_ref[kj], p.astype(BF16))
                m_ref[j:j + 1, :] = m_new
            return carry

        lax.fori_loop(0, nk, body, 0)

    lv = lam_ref[...]
    lam = (jnp.exp(jnp.sum(lv[0:1] * lv[1:2], axis=1, keepdims=True))
           - jnp.exp(jnp.sum(lv[2:3] * lv[3:4], axis=1, keepdims=True)) + lambda_init)
    l1 = jnp.sum(l8_ref[0], axis=0, keepdims=True)
    l2 = jnp.sum(l8_ref[1], axis=0, keepdims=True)
    o = acc_ref[0] * (1.0 / l1) - lam * (acc_ref[1] * (1.0 / l2))
    on = o * lax.rsqrt(jnp.mean(o * o, axis=0, keepdims=True) + EPS) * g_ref[...]
    on = on * (1.0 - lambda_init)
    o_ref[...] = on.T.astype(BF16)


def _attn(qt, k, vt, bias, rel_bias, lam_vecs, subln_g, lambda_init):
    B, nb, n, t = qt.shape
    S = k.shape[1]
    assert nb % 2 == 0 and nb > NEAR_STEPS, nb
    assert t >= MAX_DIST
    dh2 = 2 * DH_B
    return pl.pallas_call(
        functools.partial(_attn_kernel, t=t, nk=nb, lambda_init=lambda_init),
        grid=(B, HB, nb),
        in_specs=[
            pl.BlockSpec((None, None, dh2, t), lambda b, h, i: (b, i, h, 0)),
            pl.BlockSpec((None, S, dh2), lambda b, h, i: (b, 0, h)),
            pl.BlockSpec((None, nb, dh2, t), lambda b, h, i: (b, 0, h, 0)),
            pl.BlockSpec((None, NEAR_STEPS, t, t), lambda b, h, i: (h, 0, 0, 0)),
            pl.BlockSpec(memory_space=pltpu.SMEM),
            _resident(lam_vecs.shape),
            _resident(subln_g.shape),
        ],
        out_specs=pl.BlockSpec((None, t, dh2), lambda b, h, i: (b, i, h)),
        out_shape=jax.ShapeDtypeStruct((B, S, n), BF16),
        scratch_shapes=[
            pltpu.VMEM((dh2, t), BF16),
            pltpu.VMEM((dh2, t), BF16),
            pltpu.VMEM((2, t), F32),
            pltpu.VMEM((2, dh2, t), F32),
            pltpu.VMEM((2, 8, t), F32),
            pltpu.VMEM((t, t), F32),
            pltpu.VMEM((2, t, t), BF16),
            pltpu.VMEM((2, t, t), BF16),
            pltpu.VMEM((2, 128), F32),
        ],
        compiler_params=_params("parallel", "parallel", "arbitrary"),
        name="diff_attn",
    )(qt, k, vt, bias, rel_bias, lam_vecs, subln_g)


def _out_b_kernel(o_ref, x_ref, w_ref, out_ref):
    out_ref[...] = x_ref[...] + _dot(o_ref[...], w_ref[...])


def _out_b(o, x, w_out):
    B, S, D = x.shape
    ts = min(TOKEN_TILE, S)
    tile = lambda n: pl.BlockSpec((None, ts, n), lambda b, i: (b, i, 0))
    return pl.pallas_call(
        _out_b_kernel,
        grid=(B, S // ts),
        in_specs=[tile(o.shape[-1]), tile(D), _resident(w_out.shape)],
        out_specs=tile(D),
        out_shape=jax.ShapeDtypeStruct((B, S, D), F32),
        compiler_params=_params("parallel", "parallel"),
        name="out_b",
    )(o, x, w_out)


def _mem_kv_kernel(mem_ref, g_ref, w_ref, kv_ref):
    mn = _rms(mem_ref[...], g_ref[...]).astype(BF16)
    kv_ref[...] = _dot(mn, w_ref[...]).astype(BF16)


def _mem_kv(mem, g, w_kv):
    B, M, D = mem.shape
    n = w_kv.shape[1]
    return pl.pallas_call(
        _mem_kv_kernel,
        grid=(B,),
        in_specs=[pl.BlockSpec((None, M, D), lambda b: (b, 0, 0)), _resident((1, D)),
                  _resident(w_kv.shape)],
        out_specs=pl.BlockSpec((None, M, n), lambda b: (b, 0, 0)),
        out_shape=jax.ShapeDtypeStruct((B, M, n), BF16),
        compiler_params=_params("parallel"),
        name="mem_kv",
    )(mem, g, w_kv)


def _cross_kernel(x_ref, g_ref, wq_ref, kv_ref, wo_ref, out_ref):
    x = x_ref[...]
    D = x.shape[-1]
    dc = D // HC
    xn = _rms(x, g_ref[...]).astype(BF16)
    q = _dot(xn, wq_ref[...])
    parts = []
    for h in range(HC):
        qh = (q[:, h * dc:(h + 1) * dc] * (dc ** -0.5)).astype(BF16)
        kh = kv_ref[:, h * dc:(h + 1) * dc]
        vh = kv_ref[:, D + h * dc:D + (h + 1) * dc]
        s = lax.dot_general(qh, kh, NT_DIMS, preferred_element_type=F32)
        p = jnp.exp(s - jnp.max(s, axis=-1, keepdims=True))
        inv = 1.0 / jnp.sum(p, axis=-1, keepdims=True)
        parts.append((_dot(p.astype(BF16), vh) * inv).astype(BF16))
    out_ref[...] = x + _dot(jnp.concatenate(parts, axis=1), wo_ref[...])


def _cross(x, g, w_q, kv, w_out):
    B, S, D = x.shape
    M = kv.shape[1]
    ts = min(TOKEN_TILE, S)
    tile = pl.BlockSpec((None, ts, D), lambda b, i: (b, i, 0))
    return pl.pallas_call(
        _cross_kernel,
        grid=(B, S // ts),
        in_specs=[tile, _resident((1, D)), _resident(w_q.shape),
                  pl.BlockSpec((None, M, 2 * D), lambda b, i: (b, 0, 0)),
                  _resident(w_out.shape)],
        out_specs=tile,
        out_shape=jax.ShapeDtypeStruct((B, S, D), F32),
        compiler_params=_params("parallel", "parallel"),
        name="cross_attn",
    )(x, g, w_q, kv, w_out)


def _mlp_kernel(x_ref, g_ref, w1_ref, w2_ref, gf_ref, out_ref, *, final_norm):
    x = x_ref[...]
    D = x.shape[-1]
    xn = _rms(x, g_ref[...]).astype(BF16)
    acc = x
    for c in range(w1_ref.shape[1] // D):
        hcol = _dot(xn, w1_ref[:, c * D:(c + 1) * D])
        hcol = jnp.square(jnp.maximum(hcol, 0.0)).astype(BF16)
        acc = acc + _dot(hcol, w2_ref[c * D:(c + 1) * D, :])
    out_ref[...] = _rms(acc, gf_ref[...]) if final_norm else acc


def _mlp(x, g, w1, w2, g_final, final_norm):
    B, S, D = x.shape
    ts = min(TOKEN_TILE, S)
    tile = pl.BlockSpec((None, ts, D), lambda b, i: (b, i, 0))
    return pl.pallas_call(
        functools.partial(_mlp_kernel, final_norm=final_norm),
        grid=(B, S // ts),
        in_specs=[tile, _resident((1, D)), _resident(w1.shape), _resident(w2.shape),
                  _resident((1, D))],
        out_specs=tile,
        out_shape=jax.ShapeDtypeStruct((B, S, D), F32),
        compiler_params=_params("parallel", "parallel"),
        name="mlp",
    )(x, g, w1, w2, g_final)


def _prepare(g_mix, g_cross, g_mem, g_mlp, g_final, a_w_in, a_w_gate, a_b_gate, a_norm_g,
             a_w_out, b_w_qkv, b_lambda, b_subln_g, b_w_out, rel_bias, c_w_q, c_w_kv,
             c_w_out, f_w1, f_w2):
    row = lambda v: v.reshape(1, -1).astype(F32)
    nq = 2 * HB * DH_B
    wqkv = b_w_qkv[0]
    return dict(
        g_mix=[row(g) for g in g_mix], g_cross=[row(g) for g in g_cross],
        g_mem=[row(g) for g in g_mem], g_mlp=[row(g) for g in g_mlp], g_final=row(g_final),
        a_w_in=a_w_in[0].astype(BF16), a_w_gate_t=a_w_gate[0].T.astype(BF16),
        a_b_gate=a_b_gate[0].reshape(-1, 1).astype(F32), a_norm_g=row(a_norm_g[0]),
        a_w_out=a_w_out[0].astype(BF16),
        b_w_q_t=wqkv[:, :nq].T.astype(BF16), b_w_k=wqkv[:, nq:2 * nq].astype(BF16),
        b_w_v_t=wqkv[:, 2 * nq:].T.astype(BF16),
        b_lambda=b_lambda[0].astype(F32), b_subln_g=b_subln_g[0].reshape(-1, 1).astype(F32),
        b_w_out=b_w_out[0].astype(BF16),
        rel_bias=rel_bias.astype(F32),
        c_w_q=[w.astype(BF16) for w in c_w_q], c_w_kv=[w.astype(BF16) for w in c_w_kv],
        c_w_out=[w.astype(BF16) for w in c_w_out],
        f_w1=[w.astype(BF16) for w in f_w1], f_w2=[w.astype(BF16) for w in f_w2],
    )


def _encoder(x, mem, p, bias, t):
    depth = len(p["g_mix"])
    for i in range(depth):
        if i % 2 == 0:
            qkv, o, gt = _proj_a(x, p["g_mix"][i], p["a_w_in"], p["a_w_gate_t"], p["a_b_gate"])
            hf, hb = _mlstm(qkv, gt)
            x = _post_a(hf, hb, o, x, p["a_norm_g"], p["a_w_out"])
        else:
            qt, k, vt = _proj_b(x, p["g_mix"][i], p["b_w_q_t"], p["b_w_k"], p["b_w_v_t"], t)
            o = _attn(qt, k, vt, bias, p["rel_bias"], p["b_lambda"], p["b_subln_g"], _lambda_init(i))
            x = _out_b(o, x, p["b_w_out"])
        kv = _mem_kv(mem, p["g_mem"][i], p["c_w_kv"][i])
        x = _cross(x, p["g_cross"][i], p["c_w_q"][i], kv, p["c_w_out"][i])
        x = _mlp(x, p["g_mlp"][i], p["f_w1"][i], p["f_w2"][i], p["g_final"], i == depth - 1)
    return x


def kernel(x_prompt, x_sample, mem_prompt, mem_sample, g_mix, g_cross, g_mem, g_mlp, g_final, a_w_in, a_w_gate, a_b_gate, a_norm_g, a_w_out, b_w_qkv, b_lambda, b_subln_g, b_w_out, rel_bias, c_w_q, c_w_kv, c_w_out, f_w1, f_w2):
    p = _prepare(g_mix, g_cross, g_mem, g_mlp, g_final, a_w_in, a_w_gate, a_b_gate, a_norm_g,
                 a_w_out, b_w_qkv, b_lambda, b_subln_g, b_w_out, rel_bias, c_w_q, c_w_kv,
                 c_w_out, f_w1, f_w2)
    t = min(TOKEN_TILE, x_prompt.shape[1], x_sample.shape[1])
    bias = _bias_tiles(p["rel_bias"], t)
    y_prompt = _encoder(x_prompt, mem_prompt, p, bias, t)
    y_sample = _encoder(x_sample, mem_sample, p, bias, t)
    return (y_prompt, y_sample)
```

```python
import functools
import math

import jax
import jax.numpy as jnp
from jax import lax
from jax.experimental import pallas as pl
from jax.experimental.pallas import tpu as pltpu

EPS = 1e-6
HA = 4
DK_A = 128
DV_A = 256
CHUNK = 128
HB = 8
DH_B = 64
N_BUCKETS = 32
MAX_DIST = 128
HC = 4

F32 = jnp.float32
BF16 = jnp.bfloat16

VMEM_LIMIT_BYTES = 56 * 1024 * 1024
TOKEN_TILE = 512
NT_DIMS = (((1,), (1,)), ((), ()))
LOG2E = math.log2(math.e)
NEAR_STEPS = 3
FAR_UNROLL_MAX = 14
NORM_MARGIN = 1.01
BOUND_SHIFT_MAX_SPREAD = 100.0


def _lambda_init(layer):
    return 0.8 - 0.6 * math.exp(-0.3 * layer)


def _params(*sem):
    return pltpu.CompilerParams(dimension_semantics=sem, vmem_limit_bytes=VMEM_LIMIT_BYTES)


def _resident(shape):
    zeros = (0,) * len(shape)
    return pl.BlockSpec(shape, lambda *_: zeros, pipeline_mode=pl.Buffered(1))


def _rms(x, g):
    return x * lax.rsqrt(jnp.mean(x * x, axis=-1, keepdims=True) + EPS) * g


def _dot(a, b):
    return jnp.dot(a, b, preferred_element_type=F32)


def _proj_a_kernel(x_ref, g_ref, w_ref, wgt_ref, bg_ref, qkv_ref, o_ref, gt_ref):
    xn = _rms(x_ref[...], g_ref[...]).astype(BF16)
    nqk = HA * DK_A
    for c in range(4):
        y = _dot(xn, w_ref[:, c * nqk:(c + 1) * nqk])
        if c == 1:
            y = y * (DK_A ** -0.5)
        qkv_ref[:, c * nqk:(c + 1) * nqk] = y.astype(BF16)
    for c in range(2):
        o_ref[:, c * nqk:(c + 1) * nqk] = _dot(xn, w_ref[:, (4 + c) * nqk:(5 + c) * nqk])
    gt = lax.dot_general(wgt_ref[...], xn, NT_DIMS, preferred_element_type=F32)
    gt_ref[...] = gt + bg_ref[...]


def _proj_a(x, g, w_in, w_gate_t, b_gate):
    B, S, D = x.shape
    ts = min(TOKEN_TILE, S)
    n_qkv = 2 * HA * DK_A + HA * DV_A
    n_o = HA * DV_A
    return pl.pallas_call(
        _proj_a_kernel,
        grid=(B, S // ts),
        in_specs=[
            pl.BlockSpec((None, ts, D), lambda b, i: (b, i, 0)),
            _resident((1, D)),
            _resident(w_in.shape),
            _resident(w_gate_t.shape),
            _resident(b_gate.shape),
        ],
        out_specs=[
            pl.BlockSpec((None, ts, n_qkv), lambda b, i: (b, i, 0)),
            pl.BlockSpec((None, ts, n_o), lambda b, i: (b, i, 0)),
            pl.BlockSpec((None, 4 * HA, ts), lambda b, i: (b, 0, i)),
        ],
        out_shape=[
            jax.ShapeDtypeStruct((B, S, n_qkv), BF16),
            jax.ShapeDtypeStruct((B, S, n_o), F32),
            jax.ShapeDtypeStruct((B, 4 * HA, S), F32),
        ],
        compiler_params=_params("parallel", "parallel"),
        name="proj_a",
    )(x, g, w_in, w_gate_t, b_gate)


def _log_sigmoid(x):
    return jnp.minimum(x, 0.0) - jnp.log1p(jnp.exp(-jnp.abs(x)))


def _mlstm_kernel(qf_ref, kf_ref, vf_ref, qb_ref, kb_ref, vb_ref, gf_ref, gb_ref,
                  hf_ref, hb_ref, c_ref, n_ref, m_ref, t_ref):
    L = CHUNK

    @pl.when(pl.program_id(1) == 0)
    def _():
        c_ref[...] = jnp.zeros(c_ref.shape, F32)
        n_ref[...] = jnp.zeros(n_ref.shape, F32)
        m_ref[...] = jnp.zeros(m_ref.shape, F32)
        t_ref[...] = jnp.zeros(t_ref.shape, F32)

    gf = gf_ref[...]
    gb = gb_ref[...]
    row8 = lax.broadcasted_iota(jnp.int32, (2 * HA, L), 0)
    lane8 = lax.broadcasted_iota(jnp.int32, (2 * HA, L), 1)
    is_f = row8 < HA
    ig8 = jnp.where(is_f, gf, pltpu.roll(gb, HA, 0))
    fg8 = jnp.where(is_f, pltpu.roll(gf, HA, 0), gb)
    lf8 = _log_sigmoid(fg8)

    b8 = lf8
    d = 1
    while d < L:
        fwd = jnp.where(lane8 >= d, pltpu.roll(b8, d, 1), 0.0)
        bwd = jnp.where(lane8 < L - d, pltpu.roll(b8, L - d, 1), 0.0)
        b8 = b8 + jnp.where(is_f, fwd, bwd)
        d *= 2

    bl8 = jnp.where(is_f[:, 0:1], b8[:, L - 1:L], b8[:, 0:1])
    m_old8 = m_ref[:, 0:1]
    a8 = ig8 - b8
    wlog8 = bl8 + a8
    m_new8 = jnp.maximum(bl8 + m_old8, jnp.max(wlog8, axis=1, keepdims=True))
    w8 = jnp.exp(wlog8 - m_new8)
    decay8 = jnp.exp(bl8 + m_old8 - m_new8)

    t_ref[0:2 * HA, :] = b8
    t_ref[2 * HA:4 * HA, :] = w8
    tt = t_ref[...].T

    ti = lax.broadcasted_iota(jnp.int32, (L, L), 0)
    si = lax.broadcasted_iota(jnp.int32, (L, L), 1)

    for r in range(2 * HA):
        h = r % HA
        if r < HA:
            q_ref, k_ref, v_ref, out_ref, mask = qf_ref, kf_ref, vf_ref, hf_ref, si <= ti
        else:
            q_ref, k_ref, v_ref, out_ref, mask = qb_ref, kb_ref, vb_ref, hb_ref, si >= ti
        q = q_ref[:, h * DK_A:(h + 1) * DK_A]
        k = k_ref[:, h * DK_A:(h + 1) * DK_A]
        v = v_ref[:, h * DV_A:(h + 1) * DV_A]
        b_col = tt[:, r:r + 1]
        w_col = tt[:, 2 * HA + r:2 * HA + r + 1]
        m_old = m_old8[r:r + 1, :]

        dlog = jnp.where(mask, b_col + a8[r:r + 1, :], -jnp.inf)
        inter = b_col + m_old
        m_t = jnp.maximum(inter, jnp.max(dlog, axis=1, keepdims=True))
        s = lax.dot_general(q, k, NT_DIMS, preferred_element_type=F32) * jnp.exp(dlog - m_t)
        iw = jnp.exp(inter - m_t)
        c_old = c_ref[r]
        n_old = n_ref[r:r + 1, :]
        num = iw * _dot(q, c_old.astype(BF16)) + _dot(s.astype(BF16), v)
        den = (iw * jnp.sum(q.astype(F32) * n_old, axis=1, keepdims=True)
               + jnp.sum(s, axis=1, keepdims=True))
        inv = 1.0 / jnp.maximum(jnp.abs(den), jnp.exp(-m_t))
        out_ref[:, h * DV_A:(h + 1) * DV_A] = num * inv

        wk = w_col * k.astype(F32)
        dec = decay8[r:r + 1, :]
        n_ref[r:r + 1, :] = dec * n_old + jnp.sum(wk, axis=0, keepdims=True)
        c_ref[r] = dec * c_old + _dot(wk.T.astype(BF16), v)

    m_ref[...] = jnp.broadcast_to(m_new8, m_ref.shape)


def _mlstm(qkv, gt):
    B, S, _ = qkv.shape
    L = CHUNK
    nc = S // L
    nqk = HA * DK_A
    nv = HA * DV_A

    def fwd(col):
        return lambda b, c: (b, c, col)

    def bwd(col):
        return lambda b, c: (b, nc - 1 - c, col)

    return pl.pallas_call(
        _mlstm_kernel,
        grid=(B, nc),
        in_specs=[
            pl.BlockSpec((None, L, nqk), fwd(0)),
            pl.BlockSpec((None, L, nqk), fwd(1)),
            pl.BlockSpec((None, L, nv), fwd(1)),
            pl.BlockSpec((None, L, nqk), bwd(0)),
            pl.BlockSpec((None, L, nqk), bwd(1)),
            pl.BlockSpec((None, L, nv), bwd(1)),
            pl.BlockSpec((None, 2 * HA, L), lambda b, c: (b, 0, c)),
            pl.BlockSpec((None, 2 * HA, L), lambda b, c: (b, 1, nc - 1 - c)),
        ],
        out_specs=[
            pl.BlockSpec((None, L, nv), fwd(0)),
            pl.BlockSpec((None, L, nv), bwd(0)),
        ],
        out_shape=[jax.ShapeDtypeStruct((B, S, nv), F32)] * 2,
        scratch_shapes=[
            pltpu.VMEM((2 * HA, DK_A, DV_A), F32),
            pltpu.VMEM((2 * HA, DK_A), F32),
            pltpu.VMEM((2 * HA, 128), F32),
            pltpu.VMEM((128, L), F32),
        ],
        compiler_params=_params("parallel", "arbitrary"),
        name="mlstm",
    )(qkv, qkv, qkv, qkv, qkv, qkv, gt, gt)


def _post_a_kernel(hf_ref, hb_ref, o_ref, x_ref, ng_ref, w_ref, out_ref):
    parts = []
    for h in range(HA):
        sl = slice(h * DV_A, (h + 1) * DV_A)
        hh = hf_ref[:, sl] + hb_ref[:, sl]
        hn = hh * lax.rsqrt(jnp.mean(hh * hh, axis=-1, keepdims=True) + EPS)
        hn = hn * ng_ref[:, sl] * jax.nn.sigmoid(o_ref[:, sl])
        parts.append(hn.astype(BF16))
    out_ref[...] = x_ref[...] + _dot(jnp.concatenate(parts, axis=1), w_ref[...])


def _post_a(hf, hb, o, x, norm_g, w_out):
    B, S, D = x.shape
    nv = hf.shape[-1]
    ts = min(TOKEN_TILE, S)
    tile = lambda n: pl.BlockSpec((None, ts, n), lambda b, i: (b, i, 0))
    return pl.pallas_call(
        _post_a_kernel,
        grid=(B, S // ts),
        in_specs=[tile(nv), tile(nv), tile(nv), tile(D), _resident((1, nv)), _resident(w_out.shape)],
        out_specs=tile(D),
        out_shape=jax.ShapeDtypeStruct((B, S, D), F32),
        compiler_params=_params("parallel", "parallel"),
        name="post_a",
    )(hf, hb, o, x, norm_g, w_out)


def _proj_b_kernel(x_ref, g_ref, wqt_ref, wk_ref, wvt_ref, qt_ref, k_ref, vt_ref):
    xn = _rms(x_ref[...], g_ref[...]).astype(BF16)
    qt = lax.dot_general(wqt_ref[...], xn, NT_DIMS, preferred_element_type=F32)
    qt_ref[...] = (qt * (DH_B ** -0.5 * LOG2E)).astype(BF16)
    vt = lax.dot_general(wvt_ref[...], xn, NT_DIMS, preferred_element_type=F32)
    vt_ref[...] = vt.astype(BF16)
    k_ref[...] = _dot(xn, wk_ref[...]).astype(BF16)


def _proj_b(x, g, w_q_t, w_k, w_v_t, t):
    B, S, D = x.shape
    nb = S // t
    n = w_k.shape[1]
    return pl.pallas_call(
        _proj_b_kernel,
        grid=(B, nb),
        in_specs=[
            pl.BlockSpec((None, t, D), lambda b, i: (b, i, 0)),
            _resident((1, D)),
            _resident(w_q_t.shape),
            _resident(w_k.shape),
            _resident(w_v_t.shape),
        ],
        out_specs=[
            pl.BlockSpec((None, None, n, t), lambda b, i: (b, i, 0, 0)),
            pl.BlockSpec((None, t, n), lambda b, i: (b, i, 0)),
            pl.BlockSpec((None, None, n, t), lambda b, i: (b, i, 0, 0)),
        ],
        out_shape=[
            jax.ShapeDtypeStruct((B, nb, n, t), BF16),
            jax.ShapeDtypeStruct((B, S, n), BF16),
            jax.ShapeDtypeStruct((B, nb, n, t), BF16),
        ],
        compiler_params=_params("parallel", "parallel"),
        name="proj_b",
    )(x, g, w_q_t, w_k, w_v_t)


def _bias_kernel(tab_ref, out_ref, *, t):
    h = pl.program_id(0)
    off = (pl.program_id(1) - 1) * t
    ki = lax.broadcasted_iota(jnp.int32, (t, t), 0)
    qi = lax.broadcasted_iota(jnp.int32, (t, t), 1)
    rp = off + ki - qi
    half = N_BUCKETS // 2
    max_exact = half // 2
    ret = (rp > 0).astype(jnp.int32) * half
    n = jnp.abs(rp)
    nf = jnp.maximum(n, 1).astype(F32)
    large = max_exact + (jnp.log(nf / max_exact) / math.log(MAX_DIST / max_exact)
                         * (half - max_exact)).astype(jnp.int32)
    large = jnp.minimum(large, half - 1)
    bucket = ret + jnp.where(n < max_exact, n, large)
    acc = jnp.zeros((t, t), F32)
    for b in range(N_BUCKETS):
        acc = jnp.where(bucket == b, tab_ref[b, h], acc)
    out_ref[...] = acc * LOG2E


def _bias_tiles(rel_bias, t):
    return pl.pallas_call(
        functools.partial(_bias_kernel, t=t),
        grid=(HB, 3),
        in_specs=[pl.BlockSpec(memory_space=pltpu.SMEM)],
        out_specs=pl.BlockSpec((None, None, t, t), lambda h, j: (h, j, 0, 0)),
        out_shape=jax.ShapeDtypeStruct((HB, 3, t, t), F32),
        compiler_params=_params("parallel", "parallel"),
        name="bias_tiles",
    )(rel_bias)


def _attn_kernel(qt_ref, k_ref, vt_ref, bias_ref, tab_ref, lam_ref, g_ref, o_ref,
                 q1_ref, q2_ref, m_ref, acc_ref, l8_ref, s_ref, p0_ref, p1_ref, kn_ref,
                 *, t, nk, lambda_init):
    h = pl.program_id(1)
    qi = pl.program_id(2)
    p_refs = (p0_ref, p1_ref)
    dv = 2 * DH_B

    @pl.when(qi == 0)
    def _():
        first_half = lax.broadcasted_iota(jnp.int32, (t, dv), 1) < DH_B

        def body(i, carry):
            kf = k_ref[pl.ds(pl.multiple_of(i * t, t), t), :].astype(F32)
            sq = kf * kf
            n1 = jnp.sum(jnp.where(first_half, sq, 0.0), axis=1, keepdims=True)
            n2 = jnp.sum(jnp.where(first_half, 0.0, sq), axis=1, keepdims=True)
            return (jnp.maximum(carry[0], jnp.max(n1, axis=0, keepdims=True)),
                    jnp.maximum(carry[1], jnp.max(n2, axis=0, keepdims=True)))

        zero = jnp.zeros((1, 1), F32)
        n1, n2 = lax.fori_loop(0, nk, body, (zero, zero))
        kn_ref[0:1, :] = jnp.broadcast_to(n1, (1, kn_ref.shape[1]))
        kn_ref[1:2, :] = jnp.broadcast_to(n2, (1, kn_ref.shape[1]))

    qt = qt_ref[...]
    rows = lax.broadcasted_iota(jnp.int32, qt.shape, 0)
    q1_ref[...] = jnp.where(rows < DH_B, qt, jnp.zeros_like(qt))
    q2_ref[...] = jnp.where(rows >= DH_B, qt, jnp.zeros_like(qt))
    acc_ref[...] = jnp.zeros(acc_ref.shape, F32)
    l8_ref[...] = jnp.zeros(l8_ref.shape, F32)

    c_left = tab_ref[N_BUCKETS // 2 - 1, h] * LOG2E
    c_right = tab_ref[N_BUCKETS - 1, h] * LOG2E
    b_max = tab_ref[0, h]
    b_min = tab_ref[0, h]
    for b in range(1, N_BUCKETS):
        b_max = jnp.maximum(b_max, tab_ref[b, h])
        b_min = jnp.minimum(b_min, tab_ref[b, h])
    b_max = b_max * LOG2E
    b_min = b_min * LOG2E

    def is_near(kj):
        return jnp.abs(kj - qi) <= 1

    def block_of(n):
        x = qi - 1 + n
        return jnp.where(x < 0, x + nk, jnp.where(x >= nk, x - nk, x))

    def key_block(kj):
        return k_ref[pl.ds(pl.multiple_of(kj * t, t), t), :]

    spread = jnp.zeros((1, t), F32)
    for j, q_ref in enumerate((q1_ref, q2_ref)):
        qf = q_ref[...].astype(F32)
        qn2 = jnp.sum(qf * qf, axis=0, keepdims=True)
        bound = jnp.sqrt(qn2 * kn_ref[j:j + 1, 0:1]) * NORM_MARGIN + 1.0
        m_ref[j:j + 1, :] = bound + b_max
        spread = jnp.maximum(spread, 2.0 * bound + (b_max - b_min))
    safe = jnp.max(spread) <= BOUND_SHIFT_MAX_SPREAD

    def bound_ab(n, par, maybe_near):
        kj = block_of(n)
        kb = key_block(kj)
        c = jnp.where(kj < qi, c_left, c_right)
        if maybe_near:
            near = is_near(kj)
            c = jnp.where(near, 0.0, c)
            w_bias = jnp.where(near, 1.0, 0.0)
            bt = bias_ref[jnp.clip(kj - qi + 1, 0, NEAR_STEPS - 1)] * w_bias
        for j, q_ref in enumerate((q1_ref, q2_ref)):
            s = _dot(kb, q_ref[...])
            if maybe_near:
                s = s + bt
            p = jnp.exp2(s - (m_ref[j:j + 1, :] - c))
            l8_ref[j] = l8_ref[j] + p.reshape(t // 8, 8, t).sum(axis=0)
            p_refs[par][j] = p.astype(BF16)

    def bound_c(n, par):
        vt = vt_ref[block_of(n)]
        for j in range(2):
            acc_ref[j] = acc_ref[j] + _dot(vt, p_refs[par][j])

    @pl.when(safe)
    def _():
        def steps(n0, count, maybe_near):
            for u in range(count):
                par = (n0 + u) % 2 if isinstance(n0, int) else (NEAR_STEPS + u) % 2
                bound_ab(n0 + u + 1, 1 - par, maybe_near)
                bound_c(n0 + u, par)

        n_far = nk - 1 - NEAR_STEPS
        unroll = max(u for u in range(2, FAR_UNROLL_MAX + 1, 2) if n_far % u == 0)

        def body(i, carry):
            steps(NEAR_STEPS + unroll * i, unroll, False)
            return carry

        bound_ab(0, 0, True)
        steps(0, NEAR_STEPS - 1, True)
        steps(NEAR_STEPS - 1, 1, False)
        lax.fori_loop(0, n_far // unroll, body, 0)
        bound_c(nk - 1, (nk - 1) % 2)

    @pl.when(jnp.logical_not(safe))
    def _():
        m_ref[...] = jnp.full(m_ref.shape, -jnp.inf, F32)

        def body(n, carry):
            kj = block_of(n)
            kb = key_block(kj)
            near = is_near(kj)
            c = jnp.where(near, 0.0, jnp.where(kj < qi, c_left, c_right))
            for j, q_ref in enumerate((q1_ref, q2_ref)):
                s_ref[...] = _dot(kb, q_ref[...])

                @pl.when(near)
                def _():
                    s_ref[...] = s_ref[...] + bias_ref[jnp.clip(kj - qi + 1, 0, NEAR_STEPS - 1)]

                s = s_ref[...]
                m_old = m_ref[j:j + 1, :]
                m_new = jnp.maximum(m_old, jnp.max(s, axis=0, keepdims=True) + c)
                p = jnp.exp2(s - (m_new - c))
                alpha = jnp.exp2(m_old - m_new)
                l8_ref[j] = alpha * l8_ref[j] + p.reshape(t // 8, 8, t).sum(axis=0)
                acc_ref[j] = alpha * acc_ref[j] + _dot(vt_ref[kj], p.astype(BF16))
                m_ref[j:j + 1, :] = m_new
            return carry

        lax.fori_loop(0, nk, body, 0)

    lv = lam_ref[...]
    lam = (jnp.exp(jnp.sum(lv[0:1] * lv[1:2], axis=1, keepdims=True))
           - jnp.exp(jnp.sum(lv[2:3] * lv[3:4], axis=1, keepdims=True)) + lambda_init)
    l1 = jnp.sum(l8_ref[0], axis=0, keepdims=True)
    l2 = jnp.sum(l8_ref[1], axis=0, keepdims=True)
    o = acc_ref[0] * (1.0 / l1) - lam * (acc_ref[1] * (1.0 / l2))
    on = o * lax.rsqrt(jnp.mean(o * o, axis=0, keepdims=True) + EPS) * g_ref[...]
    on = on * (1.0 - lambda_init)
    o_ref[...] = on.T.astype(BF16)


def _attn(qt, k, vt, bias, rel_bias, lam_vecs, subln_g, lambda_init):
    B, nb, n, t = qt.shape
    S = k.shape[1]
    assert nb % 2 == 0 and nb > NEAR_STEPS, nb
    assert t >= MAX_DIST
    dh2 = 2 * DH_B
    return pl.pallas_call(
        functools.partial(_attn_kernel, t=t, nk=nb, lambda_init=lambda_init),
        grid=(B, HB, nb),
        in_specs=[
            pl.BlockSpec((None, None, dh2, t), lambda b, h, i: (b, i, h, 0)),
            pl.BlockSpec((None, S, dh2), lambda b, h, i: (b, 0, h)),
            pl.BlockSpec((None, nb, dh2, t), lambda b, h, i: (b, 0, h, 0)),
            pl.BlockSpec((None, NEAR_STEPS, t, t), lambda b, h, i: (h, 0, 0, 0)),
            pl.BlockSpec(memory_space=pltpu.SMEM),
            _resident(lam_vecs.shape),
            _resident(subln_g.shape),
        ],
        out_specs=pl.BlockSpec((None, t, dh2), lambda b, h, i: (b, i, h)),
        out_shape=jax.ShapeDtypeStruct((B, S, n), BF16),
        scratch_shapes=[
            pltpu.VMEM((dh2, t), BF16),
            pltpu.VMEM((dh2, t), BF16),
            pltpu.VMEM((2, t), F32),
            pltpu.VMEM((2, dh2, t), F32),
            pltpu.VMEM((2, 8, t), F32),
            pltpu.VMEM((t, t), F32),
            pltpu.VMEM((2, t, t), BF16),
            pltpu.VMEM((2, t, t), BF16),
            pltpu.VMEM((2, 128), F32),
        ],
        compiler_params=_params("parallel", "parallel", "arbitrary"),
        name="diff_attn",
    )(qt, k, vt, bias, rel_bias, lam_vecs, subln_g)


def _out_b_kernel(o_ref, x_ref, w_ref, out_ref):
    out_ref[...] = x_ref[...] + _dot(o_ref[...], w_ref[...])


def _out_b(o, x, w_out):
    B, S, D = x.shape
    ts = min(TOKEN_TILE, S)
    tile = lambda n: pl.BlockSpec((None, ts, n), lambda b, i: (b, i, 0))
    return pl.pallas_call(
        _out_b_kernel,
        grid=(B, S // ts),
        in_specs=[tile(o.shape[-1]), tile(D), _resident(w_out.shape)],
        out_specs=tile(D),
        out_shape=jax.ShapeDtypeStruct((B, S, D), F32),
        compiler_params=_params("parallel", "parallel"),
        name="out_b",
    )(o, x, w_out)


def _mem_kv_kernel(mem_ref, g_ref, w_ref, kv_ref):
    mn = _rms(mem_ref[...], g_ref[...]).astype(BF16)
    kv_ref[...] = _dot(mn, w_ref[...]).astype(BF16)


def _mem_kv(mem, g, w_kv):
    B, M, D = mem.shape
    n = w_kv.shape[1]
    return pl.pallas_call(
        _mem_kv_kernel,
        grid=(B,),
        in_specs=[pl.BlockSpec((None, M, D), lambda b: (b, 0, 0)), _resident((1, D)),
                  _resident(w_kv.shape)],
        out_specs=pl.BlockSpec((None, M, n), lambda b: (b, 0, 0)),
        out_shape=jax.ShapeDtypeStruct((B, M, n), BF16),
        compiler_params=_params("parallel"),
        name="mem_kv",
    )(mem, g, w_kv)


def _cross_kernel(x_ref, g_ref, wq_ref, kv_ref, wo_ref, out_ref):
    x = x_ref[...]
    D = x.shape[-1]
    dc = D // HC
    xn = _rms(x, g_ref[...]).astype(BF16)
    q = _dot(xn, wq_ref[...])
    parts = []
    for h in range(HC):
        qh = (q[:, h * dc:(h + 1) * dc] * (dc ** -0.5)).astype(BF16)
        kh = kv_ref[:, h * dc:(h + 1) * dc]
        vh = kv_ref[:, D + h * dc:D + (h + 1) * dc]
        s = lax.dot_general(qh, kh, NT_DIMS, preferred_element_type=F32)
        p = jnp.exp(s - jnp.max(s, axis=-1, keepdims=True))
        inv = 1.0 / jnp.sum(p, axis=-1, keepdims=True)
        parts.append((_dot(p.astype(BF16), vh) * inv).astype(BF16))
    out_ref[...] = x + _dot(jnp.concatenate(parts, axis=1), wo_ref[...])


def _cross(x, g, w_q, kv, w_out):
    B, S, D = x.shape
    M = kv.shape[1]
    ts = min(TOKEN_TILE, S)
    tile = pl.BlockSpec((None, ts, D), lambda b, i: (b, i, 0))
    return pl.pallas_call(
        _cross_kernel,
        grid=(B, S // ts),
        in_specs=[tile, _resident((1, D)), _resident(w_q.shape),
                  pl.BlockSpec((None, M, 2 * D), lambda b, i: (b, 0, 0)),
                  _resident(w_out.shape)],
        out_specs=tile,
        out_shape=jax.ShapeDtypeStruct((B, S, D), F32),
        compiler_params=_params("parallel", "parallel"),
        name="cross_attn",
    )(x, g, w_q, kv, w_out)


def _mlp_kernel(x_ref, g_ref, w1_ref, w2_ref, gf_ref, out_ref, *, final_norm):
    x = x_ref[...]
    D = x.shape[-1]
    xn = _rms(x, g_ref[...]).astype(BF16)
    acc = x
    for c in range(w1_ref.shape[1] // D):
        hcol = _dot(xn, w1_ref[:, c * D:(c + 1) * D])
        hcol = jnp.square(jnp.maximum(hcol, 0.0)).astype(BF16)
        acc = acc + _dot(hcol, w2_ref[c * D:(c + 1) * D, :])
    out_ref[...] = _rms(acc, gf_ref[...]) if final_norm else acc


def _mlp(x, g, w1, w2, g_final, final_norm):
    B, S, D = x.shape
    ts = min(TOKEN_TILE, S)
    tile = pl.BlockSpec((None, ts, D), lambda b, i: (b, i, 0))
    return pl.pallas_call(
        functools.partial(_mlp_kernel, final_norm=final_norm),
        grid=(B, S // ts),
        in_specs=[tile, _resident((1, D)), _resident(w1.shape), _resident(w2.shape),
                  _resident((1, D))],
        out_specs=tile,
        out_shape=jax.ShapeDtypeStruct((B, S, D), F32),
        compiler_params=_params("parallel", "parallel"),
        name="mlp",
    )(x, g, w1, w2, g_final)


def _prepare(g_mix, g_cross, g_mem, g_mlp, g_final, a_w_in, a_w_gate, a_b_gate, a_norm_g,
             a_w_out, b_w_qkv, b_lambda, b_subln_g, b_w_out, rel_bias, c_w_q, c_w_kv,
             c_w_out, f_w1, f_w2):
    row = lambda v: v.reshape(1, -1).astype(F32)
    nq = 2 * HB * DH_B
    wqkv = b_w_qkv[0]
    return dict(
        g_mix=[row(g) for g in g_mix], g_cross=[row(g) for g in g_cross],
        g_mem=[row(g) for g in g_mem], g_mlp=[row(g) for g in g_mlp], g_final=row(g_final),
        a_w_in=a_w_in[0].astype(BF16), a_w_gate_t=a_w_gate[0].T.astype(BF16),
        a_b_gate=a_b_gate[0].reshape(-1, 1).astype(F32), a_norm_g=row(a_norm_g[0]),
        a_w_out=a_w_out[0].astype(BF16),
        b_w_q_t=wqkv[:, :nq].T.astype(BF16), b_w_k=wqkv[:, nq:2 * nq].astype(BF16),
        b_w_v_t=wqkv[:, 2 * nq:].T.astype(BF16),
        b_lambda=b_lambda[0].astype(F32), b_subln_g=b_subln_g[0].reshape(-1, 1).astype(F32),
        b_w_out=b_w_out[0].astype(BF16),
        rel_bias=rel_bias.astype(F32),
        c_w_q=[w.astype(BF16) for w in c_w_q], c_w_kv=[w.astype(BF16) for w in c_w_kv],
        c_w_out=[w.astype(BF16) for w in c_w_out],
        f_w1=[w.astype(BF16) for w in f_w1], f_w2=[w.astype(BF16) for w in f_w2],
    )


def _encoder(x, mem, p, bias, t):
    depth = len(p["g_mix"])
    for i in range(depth):
        if i % 2 == 0:
            qkv, o, gt = _proj_a(x, p["g_mix"][i], p["a_w_in"], p["a_w_gate_t"], p["a_b_gate"])
            hf, hb = _mlstm(qkv, gt)
            x = _post_a(hf, hb, o, x, p["a_norm_g"], p["a_w_out"])
        else:
            qt, k, vt = _proj_b(x, p["g_mix"][i], p["b_w_q_t"], p["b_w_k"], p["b_w_v_t"], t)
            o = _attn(qt, k, vt, bias, p["rel_bias"], p["b_lambda"], p["b_subln_g"], _lambda_init(i))
            x = _out_b(o, x, p["b_w_out"])
        kv = _mem_kv(mem, p["g_mem"][i], p["c_w_kv"][i])
        x = _cross(x, p["g_cross"][i], p["c_w_q"][i], kv, p["c_w_out"][i])
        x = _mlp(x, p["g_mlp"][i], p["f_w1"][i], p["f_w2"][i], p["g_final"], i == depth - 1)
    return x


def kernel(x_prompt, x_sample, mem_prompt, mem_sample, g_mix, g_cross, g_mem, g_mlp, g_final, a_w_in, a_w_gate, a_b_gate, a_norm_g, a_w_out, b_w_qkv, b_lambda, b_subln_g, b_w_out, rel_bias, c_w_q, c_w_kv, c_w_out, f_w1, f_w2):
    p = _prepare(g_mix, g_cross, g_mem, g_mlp, g_final, a_w_in, a_w_gate, a_b_gate, a_norm_g,
                 a_w_out, b_w_qkv, b_lambda, b_subln_g, b_w_out, rel_bias, c_w_q, c_w_kv,
                 c_w_out, f_w1, f_w2)
    t = min(TOKEN_TILE, x_prompt.shape[1], x_sample.shape[1])
    bias = _bias_tiles(p["rel_bias"], t)
    y_prompt = _encoder(x_prompt, mem_prompt, p, bias, t)
    y_sample = _encoder(x_sample, mem_sample, p, bias, t)
    return (y_prompt, y_sample)
```

```python
import functools
import math

import jax
import jax.numpy as jnp
from jax import lax
from jax.experimental import pallas as pl
from jax.experimental.pallas import tpu as pltpu

EPS = 1e-6
HA = 4
DK_A = 128
DV_A = 256
CHUNK = 128
HB = 8
DH_B = 64
N_BUCKETS = 32
MAX_DIST = 128
HC = 4

F32 = jnp.float32
BF16 = jnp.bfloat16

VMEM_LIMIT_BYTES = 56 * 1024 * 1024
TOKEN_TILE = 512
TAIL_TILE = 512
NT_DIMS = (((1,), (1,)), ((), ()))
LOG2E = math.log2(math.e)
NEAR_STEPS = 3
FAR_UNROLL_MAX = 14
NORM_MARGIN = 1.01
BOUND_SHIFT_MAX_SPREAD = 100.0


def _lambda_init(layer):
    return 0.8 - 0.6 * math.exp(-0.3 * layer)


def _params(*sem):
    return pltpu.CompilerParams(dimension_semantics=sem, vmem_limit_bytes=VMEM_LIMIT_BYTES)


def _resident(shape):
    zeros = (0,) * len(shape)
    return pl.BlockSpec(shape, lambda *_: zeros, pipeline_mode=pl.Buffered(1))


def _rms(x, g):
    return x * lax.rsqrt(jnp.mean(x * x, axis=-1, keepdims=True) + EPS) * g


def _dot(a, b):
    return jnp.dot(a, b, preferred_element_type=F32)


def _proj_a_kernel(x_ref, g_ref, w_ref, wgt_ref, bg_ref, qkv_ref, o_ref, gt_ref):
    xn = _rms(x_ref[...], g_ref[...]).astype(BF16)
    nqk = HA * DK_A
    for c in range(4):
        y = _dot(xn, w_ref[:, c * nqk:(c + 1) * nqk])
        if c == 1:
            y = y * (DK_A ** -0.5)
        qkv_ref[:, c * nqk:(c + 1) * nqk] = y.astype(BF16)
    for c in range(2):
        o_ref[:, c * nqk:(c + 1) * nqk] = _dot(xn, w_ref[:, (4 + c) * nqk:(5 + c) * nqk])
    gt = lax.dot_general(wgt_ref[...], xn, NT_DIMS, preferred_element_type=F32)
    gt_ref[...] = gt + bg_ref[...]


def _proj_a(x, g, w_in, w_gate_t, b_gate):
    B, S, D = x.shape
    ts = min(TOKEN_TILE, S)
    n_qkv = 2 * HA * DK_A + HA * DV_A
    n_o = HA * DV_A
    return pl.pallas_call(
        _proj_a_kernel,
        grid=(B, S // ts),
        in_specs=[
            pl.BlockSpec((None, ts, D), lambda b, i: (b, i, 0)),
            _resident((1, D)),
            _resident(w_in.shape),
            _resident(w_gate_t.shape),
            _resident(b_gate.shape),
        ],
        out_specs=[
            pl.BlockSpec((None, ts, n_qkv), lambda b, i: (b, i, 0)),
            pl.BlockSpec((None, ts, n_o), lambda b, i: (b, i, 0)),
            pl.BlockSpec((None, 4 * HA, ts), lambda b, i: (b, 0, i)),
        ],
        out_shape=[
            jax.ShapeDtypeStruct((B, S, n_qkv), BF16),
            jax.ShapeDtypeStruct((B, S, n_o), F32),
            jax.ShapeDtypeStruct((B, 4 * HA, S), F32),
        ],
        compiler_params=_params("parallel", "parallel"),
        name="proj_a",
    )(x, g, w_in, w_gate_t, b_gate)


def _log_sigmoid(x):
    return jnp.minimum(x, 0.0) - jnp.log1p(jnp.exp(-jnp.abs(x)))


def _mlstm_gate_prep(gf, gb):
    L = CHUNK
    row8 = lax.broadcasted_iota(jnp.int32, (2 * HA, L), 0)
    lane8 = lax.broadcasted_iota(jnp.int32, (2 * HA, L), 1)
    is_f = row8 < HA
    ig8 = jnp.where(is_f, gf, pltpu.roll(gb, HA, 0))
    fg8 = jnp.where(is_f, pltpu.roll(gf, HA, 0), gb)
    lf8 = _log_sigmoid(fg8)

    b8 = lf8
    d = 1
    while d < L:
        fwd = jnp.where(lane8 >= d, pltpu.roll(b8, d, 1), 0.0)
        bwd = jnp.where(lane8 < L - d, pltpu.roll(b8, L - d, 1), 0.0)
        b8 = b8 + jnp.where(is_f, fwd, bwd)
        d *= 2

    bl8 = jnp.where(is_f[:, 0:1], b8[:, L - 1:L], b8[:, 0:1])
    a8 = ig8 - b8
    wlog8 = bl8 + a8
    mw8 = jnp.max(wlog8, axis=1, keepdims=True)
    return b8, a8, jnp.exp(wlog8 - mw8), bl8, mw8


def _mlstm_kernel(qf_ref, kf_ref, vf_ref, qb_ref, kb_ref, vb_ref, gf_ref, gb_ref, gfn_ref, gbn_ref,
                  hf_ref, hb_ref, c_ref, m_ref, prep_ref, sc_ref):
    L = CHUNK
    nh2 = 2 * HA

    def store_prep(gf, gb):
        b8, a8, w8, bl8, mw8 = _mlstm_gate_prep(gf, gb)
        prep_ref[0] = b8
        prep_ref[1] = a8
        prep_ref[2] = w8
        sc_ref[0:nh2, :] = jnp.broadcast_to(bl8, (nh2, sc_ref.shape[1]))
        sc_ref[nh2:2 * nh2, :] = jnp.broadcast_to(mw8, (nh2, sc_ref.shape[1]))

    @pl.when(pl.program_id(1) == 0)
    def _():
        c_ref[...] = jnp.zeros(c_ref.shape, F32)
        m_ref[...] = jnp.zeros(m_ref.shape, F32)
        store_prep(gf_ref[...], gb_ref[...])

    b8 = prep_ref[0]
    a8 = prep_ref[1]
    w8 = prep_ref[2]
    bl8 = sc_ref[0:nh2, 0:1]
    mw8 = sc_ref[nh2:2 * nh2, 0:1]
    m_old8 = m_ref[:, 0:1]
    m_new8 = jnp.maximum(bl8 + m_old8, mw8)
    decay8 = jnp.exp(bl8 + m_old8 - m_new8)
    fw8 = jnp.exp(mw8 - m_new8)

    ti = lax.broadcasted_iota(jnp.int32, (L, L), 0)
    si = lax.broadcasted_iota(jnp.int32, (L, L), 1)

    ones = jnp.ones((L, 128), BF16)

    def lanes(x, reps):
        return jnp.concatenate([x] * reps, axis=1)

    for r in range(nh2):
        h = r % HA
        if r < HA:
            q_ref, k_ref, v_ref, out_ref, mask = qf_ref, kf_ref, vf_ref, hf_ref, si <= ti
        else:
            q_ref, k_ref, v_ref, out_ref, mask = qb_ref, kb_ref, vb_ref, hb_ref, si >= ti
        q = q_ref[:, h * DK_A:(h + 1) * DK_A]
        k = k_ref[:, h * DK_A:(h + 1) * DK_A]
        v1 = jnp.concatenate([v_ref[:, h * DV_A:(h + 1) * DV_A], ones], axis=1)
        b_mat = jnp.broadcast_to(b8[r:r + 1, :], (L, L)).T
        w_mat = jnp.broadcast_to(w8[r:r + 1, :], (L, L)).T
        m_old = m_old8[r:r + 1, :]

        dlog = jnp.where(mask, b_mat + a8[r:r + 1, :], -jnp.inf)
        inter = b_mat + m_old
        m_t = jnp.maximum(inter, jnp.broadcast_to(jnp.max(dlog, axis=1, keepdims=True), (L, L)))
        s = lax.dot_general(q, k, NT_DIMS, preferred_element_type=F32) * jnp.exp(dlog - m_t)
        iw = jnp.exp(inter - m_t)
        c_old = c_ref[r]
        numden = lanes(iw, 3) * _dot(q, c_old.astype(BF16)) + _dot(s.astype(BF16), v1)
        den = numden[:, DV_A:]
        inv = 1.0 / jnp.maximum(jnp.abs(den), jnp.exp(-m_t))
        out_ref[:, h * DV_A:(h + 1) * DV_A] = numden[:, :DV_A] * lanes(inv, 2)

        wk = w_mat * k.astype(F32)
        c_ref[r] = decay8[r:r + 1, :] * c_old + fw8[r:r + 1, :] * _dot(wk.T.astype(BF16), v1)

    m_ref[...] = jnp.broadcast_to(m_new8, m_ref.shape)
    store_prep(gfn_ref[...], gbn_ref[...])


def _mlstm(qkv, gt):
    B, S, _ = qkv.shape
    L = CHUNK
    nc = S // L
    nqk = HA * DK_A
    nv = HA * DV_A

    def fwd(col):
        return lambda b, c: (b, c, col)

    def bwd(col):
        return lambda b, c: (b, nc - 1 - c, col)

    return pl.pallas_call(
        _mlstm_kernel,
        grid=(B, nc),
        in_specs=[
            pl.BlockSpec((None, L, nqk), fwd(0)),
            pl.BlockSpec((None, L, nqk), fwd(1)),
            pl.BlockSpec((None, L, nv), fwd(1)),
            pl.BlockSpec((None, L, nqk), bwd(0)),
            pl.BlockSpec((None, L, nqk), bwd(1)),
            pl.BlockSpec((None, L, nv), bwd(1)),
            pl.BlockSpec((None, 2 * HA, L), lambda b, c: (b, 0, c)),
            pl.BlockSpec((None, 2 * HA, L), lambda b, c: (b, 1, nc - 1 - c)),
            pl.BlockSpec((None, 2 * HA, L), lambda b, c: (b, 0, jnp.minimum(c + 1, nc - 1))),
            pl.BlockSpec((None, 2 * HA, L), lambda b, c: (b, 1, jnp.maximum(nc - 2 - c, 0))),
        ],
        out_specs=[
            pl.BlockSpec((None, L, nv), fwd(0)),
            pl.BlockSpec((None, L, nv), bwd(0)),
        ],
        out_shape=[jax.ShapeDtypeStruct((B, S, nv), F32)] * 2,
        scratch_shapes=[
            pltpu.VMEM((2 * HA, DK_A, DV_A + 128), F32),
            pltpu.VMEM((2 * HA, 128), F32),
            pltpu.VMEM((3, 2 * HA, L), F32),
            pltpu.VMEM((4 * HA, 128), F32),
        ],
        compiler_params=_params("parallel", "arbitrary"),
        name="mlstm",
    )(qkv, qkv, qkv, qkv, qkv, qkv, gt, gt, gt, gt)


def _proj_b_kernel(x_ref, g_ref, wqt_ref, wk_ref, wvt_ref, qt_ref, k_ref, vt_ref):
    xn = _rms(x_ref[...], g_ref[...]).astype(BF16)
    qt = lax.dot_general(wqt_ref[...], xn, NT_DIMS, preferred_element_type=F32)
    qt_ref[...] = (qt * (DH_B ** -0.5 * LOG2E)).astype(BF16)
    vt = lax.dot_general(wvt_ref[...], xn, NT_DIMS, preferred_element_type=F32)
    vt_ref[...] = vt.astype(BF16)
    k_ref[...] = _dot(xn, wk_ref[...]).astype(BF16)


def _proj_b(x, g, w_q_t, w_k, w_v_t, t):
    B, S, D = x.shape
    nb = S // t
    n = w_k.shape[1]
    return pl.pallas_call(
        _proj_b_kernel,
        grid=(B, nb),
        in_specs=[
            pl.BlockSpec((None, t, D), lambda b, i: (b, i, 0)),
            _resident((1, D)),
            _resident(w_q_t.shape),
            _resident(w_k.shape),
            _resident(w_v_t.shape),
        ],
        out_specs=[
            pl.BlockSpec((None, None, n, t), lambda b, i: (b, i, 0, 0)),
            pl.BlockSpec((None, t, n), lambda b, i: (b, i, 0)),
            pl.BlockSpec((None, None, n, t), lambda b, i: (b, i, 0, 0)),
        ],
        out_shape=[
            jax.ShapeDtypeStruct((B, nb, n, t), BF16),
            jax.ShapeDtypeStruct((B, S, n), BF16),
            jax.ShapeDtypeStruct((B, nb, n, t), BF16),
        ],
        compiler_params=_params("parallel", "parallel"),
        name="proj_b",
    )(x, g, w_q_t, w_k, w_v_t)


def _bias_kernel(tab_ref, out_ref, *, t):
    h = pl.program_id(0)
    off = (pl.program_id(1) - 1) * t
    ki = lax.broadcasted_iota(jnp.int32, (t, t), 0)
    qi = lax.broadcasted_iota(jnp.int32, (t, t), 1)
    rp = off + ki - qi
    half = N_BUCKETS // 2
    max_exact = half // 2
    ret = (rp > 0).astype(jnp.int32) * half
    n = jnp.abs(rp)
    nf = jnp.maximum(n, 1).astype(F32)
    large = max_exact + (jnp.log(nf / max_exact) / math.log(MAX_DIST / max_exact)
                         * (half - max_exact)).astype(jnp.int32)
    large = jnp.minimum(large, half - 1)
    bucket = ret + jnp.where(n < max_exact, n, large)
    acc = jnp.zeros((t, t), F32)
    for b in range(N_BUCKETS):
        acc = jnp.where(bucket == b, tab_ref[b, h], acc)
    out_ref[...] = acc * LOG2E


def _bias_tiles(rel_bias, t):
    return pl.pallas_call(
        functools.partial(_bias_kernel, t=t),
        grid=(HB, 3),
        in_specs=[pl.BlockSpec(memory_space=pltpu.SMEM)],
        out_specs=pl.BlockSpec((None, None, t, t), lambda h, j: (h, j, 0, 0)),
        out_shape=jax.ShapeDtypeStruct((HB, 3, t, t), F32),
        compiler_params=_params("parallel", "parallel"),
        name="bias_tiles",
    )(rel_bias)


def _attn_kernel(qt_ref, k_ref, vt_ref, bias_ref, tab_ref, lam_ref, g_ref, o_ref,
                 q1_ref, q2_ref, m_ref, acc_ref, l8_ref, s_ref, p0_ref, p1_ref, kn_ref,
                 *, t, nk, lambda_init):
    h = pl.program_id(1)
    qi = pl.program_id(2)
    p_refs = (p0_ref, p1_ref)
    dv = 2 * DH_B

    @pl.when(qi == 0)
    def _():
        first_half = lax.broadcasted_iota(jnp.int32, (t, dv), 1) < DH_B

        def body(i, carry):
            kf = k_ref[pl.ds(pl.multiple_of(i * t, t), t), :].astype(F32)
            sq = kf * kf
            n1 = jnp.sum(jnp.where(first_half, sq, 0.0), axis=1, keepdims=True)
            n2 = jnp.sum(jnp.where(first_half, 0.0, sq), axis=1, keepdims=True)
            return (jnp.maximum(carry[0], jnp.max(n1, axis=0, keepdims=True)),
                    jnp.maximum(carry[1], jnp.max(n2, axis=0, keepdims=True)))

        zero = jnp.zeros((1, 1), F32)
        n1, n2 = lax.fori_loop(0, nk, body, (zero, zero))
        kn_ref[0:1, :] = jnp.broadcast_to(n1, (1, kn_ref.shape[1]))
        kn_ref[1:2, :] = jnp.broadcast_to(n2, (1, kn_ref.shape[1]))

    qt = qt_ref[...]
    rows = lax.broadcasted_iota(jnp.int32, qt.shape, 0)
    q1_ref[...] = jnp.where(rows < DH_B, qt, jnp.zeros_like(qt))
    q2_ref[...] = jnp.where(rows >= DH_B, qt, jnp.zeros_like(qt))
    acc_ref[...] = jnp.zeros(acc_ref.shape, F32)
    l8_ref[...] = jnp.zeros(l8_ref.shape, F32)

    c_left = tab_ref[N_BUCKETS // 2 - 1, h] * LOG2E
    c_right = tab_ref[N_BUCKETS - 1, h] * LOG2E
    b_max = tab_ref[0, h]
    b_min = tab_ref[0, h]
    for b in range(1, N_BUCKETS):
        b_max = jnp.maximum(b_max, tab_ref[b, h])
        b_min = jnp.minimum(b_min, tab_ref[b, h])
    b_max = b_max * LOG2E
    b_min = b_min * LOG2E

    def is_near(kj):
        return jnp.abs(kj - qi) <= 1

    def block_of(n):
        x = qi - 1 + n
        return jnp.where(x < 0, x + nk, jnp.where(x >= nk, x - nk, x))

    def key_block(kj):
        return k_ref[pl.ds(pl.multiple_of(kj * t, t), t), :]

    spread = jnp.zeros((1, t), F32)
    for j, q_ref in enumerate((q1_ref, q2_ref)):
        qf = q_ref[...].astype(F32)
        qn2 = jnp.sum(qf * qf, axis=0, keepdims=True)
        bound = jnp.sqrt(qn2 * kn_ref[j:j + 1, 0:1]) * NORM_MARGIN + 1.0
        m_ref[j:j + 1, :] = bound + b_max
        spread = jnp.maximum(spread, 2.0 * bound + (b_max - b_min))
    safe = jnp.max(spread) <= BOUND_SHIFT_MAX_SPREAD

    def bound_ab(n, par, maybe_near):
        kj = block_of(n)
        kb = key_block(kj)
        c = jnp.where(kj < qi, c_left, c_right)
        if maybe_near:
            near = is_near(kj)
            c = jnp.where(near, 0.0, c)
            w_bias = jnp.where(near, 1.0, 0.0)
            bt = bias_ref[jnp.clip(kj - qi + 1, 0, NEAR_STEPS - 1)] * w_bias
        for j, q_ref in enumerate((q1_ref, q2_ref)):
            s = _dot(kb, q_ref[...])
            if maybe_near:
                s = s + bt
            p = jnp.exp2(s - (m_ref[j:j + 1, :] - c))
            l8_ref[j] = l8_ref[j] + p.reshape(t // 8, 8, t).sum(axis=0)
            p_refs[par][j] = p.astype(BF16)

    def bound_c(n, par):
        vt = vt_ref[block_of(n)]
        for j in range(2):
            acc_ref[j] = acc_ref[j] + _dot(vt, p_refs[par][j])

    @pl.when(safe)
    def _():
        def steps(n0, count, maybe_near):
            for u in range(count):
                par = (n0 + u) % 2 if isinstance(n0, int) else (NEAR_STEPS + u) % 2
                bound_ab(n0 + u + 1, 1 - par, maybe_near)
                bound_c(n0 + u, par)

        n_far = nk - 1 - NEAR_STEPS
        unroll = max(u for u in range(2, FAR_UNROLL_MAX + 1, 2) if n_far % u == 0)

        def body(i, carry):
            steps(NEAR_STEPS + unroll * i, unroll, False)
            return carry

        bound_ab(0, 0, True)
        steps(0, NEAR_STEPS - 1, True)
        steps(NEAR_STEPS - 1, 1, False)
        lax.fori_loop(0, n_far // unroll, body, 0)
        bound_c(nk - 1, (nk - 1) % 2)

    @pl.when(jnp.logical_not(safe))
    def _():
        m_ref[...] = jnp.full(m_ref.shape, -jnp.inf, F32)

        def body(n, carry):
            kj = block_of(n)
            kb = key_block(kj)
            near = is_near(kj)
            c = jnp.where(near, 0.0, jnp.where(kj < qi, c_left, c_right))
            for j, q_ref in enumerate((q1_ref, q2_ref)):
                s_ref[...] = _dot(kb, q_ref[...])

                @pl.when(near)
                def _():
                    s_ref[...] = s_ref[...] + bias_ref[jnp.clip(kj - qi + 1, 0, NEAR_STEPS - 1)]

                s = s_ref[...]
                m_old = m_ref[j:j + 1, :]
                m_new = jnp.maximum(m_old, jnp.max(s, axis=0, keepdims=True) + c)
                p = jnp.exp2(s - (m_new - c))
                alpha = jnp.exp2(m_old - m_new)
                l8_ref[j] = alpha * l8_ref[j] + p.reshape(t // 8, 8, t).sum(axis=0)
                acc_ref[j] = alpha * acc_ref[j] + _dot(vt_ref[kj], p.astype(BF16))
                m_ref[j:j + 1, :] = m_new
            return carry

        lax.fori_loop(0, nk, body, 0)

    lv = lam_ref[...]
    lam = (jnp.exp(jnp.sum(lv[0:1] * lv[1:2], axis=1, keepdims=True))
           - jnp.exp(jnp.sum(lv[2:3] * lv[3:4], axis=1, keepdims=True)) + lambda_init)
    l1 = jnp.sum(l8_ref[0], axis=0, keepdims=True)
    l2 = jnp.sum(l8_ref[1], axis=0, keepdims=True)
    o = acc_ref[0] * (1.0 / l1) - lam * (acc_ref[1] * (1.0 / l2))
    on = o * lax.rsqrt(jnp.mean(o * o, axis=0, keepdims=True) + EPS) * g_ref[...]
    on = on * (1.0 - lambda_init)
    o_ref[...] = on.T.astype(BF16)


def _attn(qt, k, vt, bias, rel_bias, lam_vecs, subln_g, lambda_init):
    B, nb, n, t = qt.shape
    S = k.shape[1]
    assert nb % 2 == 0 and nb > NEAR_STEPS, nb
    assert t >= MAX_DIST
    dh2 = 2 * DH_B
    return pl.pallas_call(
        functools.partial(_attn_kernel, t=t, nk=nb, lambda_init=lambda_init),
        grid=(B, HB, nb),
        in_specs=[
            pl.BlockSpec((None, None, dh2, t), lambda b, h, i: (b, i, h, 0)),
            pl.BlockSpec((None, S, dh2), lambda b, h, i: (b, 0, h)),
            pl.BlockSpec((None, nb, dh2, t), lambda b, h, i: (b, 0, h, 0)),
            pl.BlockSpec((None, NEAR_STEPS, t, t), lambda b, h, i: (h, 0, 0, 0)),
            pl.BlockSpec(memory_space=pltpu.SMEM),
            _resident(lam_vecs.shape),
            _resident(subln_g.shape),
        ],
        out_specs=pl.BlockSpec((None, t, dh2), lambda b, h, i: (b, i, h)),
        out_shape=jax.ShapeDtypeStruct((B, S, n), BF16),
        scratch_shapes=[
            pltpu.VMEM((dh2, t), BF16),
            pltpu.VMEM((dh2, t), BF16),
            pltpu.VMEM((2, t), F32),
            pltpu.VMEM((2, dh2, t), F32),
            pltpu.VMEM((2, 8, t), F32),
            pltpu.VMEM((t, t), F32),
            pltpu.VMEM((2, t, t), BF16),
            pltpu.VMEM((2, t, t), BF16),
            pltpu.VMEM((2, 128), F32),
        ],
        compiler_params=_params("parallel", "parallel", "arbitrary"),
        name="diff_attn",
    )(qt, k, vt, bias, rel_bias, lam_vecs, subln_g)


def _mem_kv_kernel(mem_ref, g_ref, w_ref, kv_ref):
    mn = _rms(mem_ref[...], g_ref[...]).astype(BF16)
    kv_ref[...] = _dot(mn, w_ref[...]).astype(BF16)


def _mem_kv(mem, g, w_kv):
    B, M, D = mem.shape
    n = w_kv.shape[1]
    return pl.pallas_call(
        _mem_kv_kernel,
        grid=(B,),
        in_specs=[pl.BlockSpec((None, M, D), lambda b: (b, 0, 0)), _resident((1, D)),
                  _resident(w_kv.shape)],
        out_specs=pl.BlockSpec((None, M, n), lambda b: (b, 0, 0)),
        out_shape=jax.ShapeDtypeStruct((B, M, n), BF16),
        compiler_params=_params("parallel"),
        name="mem_kv",
    )(mem, g, w_kv)


def _tail_kernel(*refs, mixer, final_norm):
    if mixer == "mlstm":
        (hf_ref, hb_ref, o_ref, x_ref, ng_ref, wm_ref, gc_ref, wq_ref, kv_ref, wo_ref,
         gm_ref, w1_ref, w2_ref, gf_ref, out_ref) = refs
        parts = []
        for h in range(HA):
            sl = slice(h * DV_A, (h + 1) * DV_A)
            hh = hf_ref[:, sl] + hb_ref[:, sl]
            hn = hh * lax.rsqrt(jnp.mean(hh * hh, axis=-1, keepdims=True) + EPS)
            hn = hn * ng_ref[:, sl] * jax.nn.sigmoid(o_ref[:, sl])
            parts.append(hn.astype(BF16))
        mixed = jnp.concatenate(parts, axis=1)
    else:
        (o_ref, x_ref, wm_ref, gc_ref, wq_ref, kv_ref, wo_ref,
         gm_ref, w1_ref, w2_ref, gf_ref, out_ref) = refs
        mixed = o_ref[...]
    x = x_ref[...] + _dot(mixed, wm_ref[...])

    D = x.shape[-1]
    dc = D // HC
    xn = _rms(x, gc_ref[...]).astype(BF16)
    q = _dot(xn, wq_ref[...])
    parts = []
    for h in range(HC):
        qh = (q[:, h * dc:(h + 1) * dc] * (dc ** -0.5)).astype(BF16)
        kh = kv_ref[:, h * dc:(h + 1) * dc]
        vh = kv_ref[:, D + h * dc:D + (h + 1) * dc]
        s = lax.dot_general(qh, kh, NT_DIMS, preferred_element_type=F32)
        p = jnp.exp(s - jnp.max(s, axis=-1, keepdims=True))
        inv = 1.0 / jnp.sum(p, axis=-1, keepdims=True)
        parts.append((_dot(p.astype(BF16), vh) * inv).astype(BF16))
    x = x + _dot(jnp.concatenate(parts, axis=1), wo_ref[...])

    xn = _rms(x, gm_ref[...]).astype(BF16)
    acc = x
    for c in range(w1_ref.shape[1] // D):
        hcol = _dot(xn, w1_ref[:, c * D:(c + 1) * D])
        hcol = jnp.square(jnp.maximum(hcol, 0.0)).astype(BF16)
        acc = acc + _dot(hcol, w2_ref[c * D:(c + 1) * D, :])
    out_ref[...] = _rms(acc, gf_ref[...]) if final_norm else acc


def _tail(mixer, mixer_inputs, x, mixer_consts, g_cross, w_q, kv, w_out, g_mlp, w1, w2, g_final,
          final_norm):
    B, S, D = x.shape
    M = kv.shape[1]
    ts = min(TAIL_TILE, S)
    tile = lambda n: pl.BlockSpec((None, ts, n), lambda b, i: (b, i, 0))
    in_specs = ([tile(a.shape[-1]) for a in mixer_inputs] + [tile(D)]
                + [_resident(c.shape) for c in mixer_consts]
                + [_resident((1, D)), _resident(w_q.shape),
                   pl.BlockSpec((None, M, 2 * D), lambda b, i: (b, 0, 0)),
                   _resident(w_out.shape), _resident((1, D)), _resident(w1.shape),
                   _resident(w2.shape), _resident((1, D))])
    return pl.pallas_call(
        functools.partial(_tail_kernel, mixer=mixer, final_norm=final_norm),
        grid=(B, S // ts),
        in_specs=in_specs,
        out_specs=tile(D),
        out_shape=jax.ShapeDtypeStruct((B, S, D), F32),
        compiler_params=_params("parallel", "parallel"),
        name="tail_" + mixer,
    )(*mixer_inputs, x, *mixer_consts, g_cross, w_q, kv, w_out, g_mlp, w1, w2, g_final)


def _prepare(g_mix, g_cross, g_mem, g_mlp, g_final, a_w_in, a_w_gate, a_b_gate, a_norm_g,
             a_w_out, b_w_qkv, b_lambda, b_subln_g, b_w_out, rel_bias, c_w_q, c_w_kv,
             c_w_out, f_w1, f_w2):
    row = lambda v: v.reshape(1, -1).astype(F32)
    nq = 2 * HB * DH_B
    wqkv = b_w_qkv[0]
    return dict(
        g_mix=[row(g) for g in g_mix], g_cross=[row(g) for g in g_cross],
        g_mem=[row(g) for g in g_mem], g_mlp=[row(g) for g in g_mlp], g_final=row(g_final),
        a_w_in=a_w_in[0].astype(BF16), a_w_gate_t=a_w_gate[0].T.astype(BF16),
        a_b_gate=a_b_gate[0].reshape(-1, 1).astype(F32), a_norm_g=row(a_norm_g[0]),
        a_w_out=a_w_out[0].astype(BF16),
        b_w_q_t=wqkv[:, :nq].T.astype(BF16), b_w_k=wqkv[:, nq:2 * nq].astype(BF16),
        b_w_v_t=wqkv[:, 2 * nq:].T.astype(BF16),
        b_lambda=b_lambda[0].astype(F32), b_subln_g=b_subln_g[0].reshape(-1, 1).astype(F32),
        b_w_out=b_w_out[0].astype(BF16),
        rel_bias=rel_bias.astype(F32),
        c_w_q=[w.astype(BF16) for w in c_w_q], c_w_kv=[w.astype(BF16) for w in c_w_kv],
        c_w_out=[w.astype(BF16) for w in c_w_out],
        f_w1=[w.astype(BF16) for w in f_w1], f_w2=[w.astype(BF16) for w in f_w2],
    )


def _encoder(x, mem, p, bias, t):
    depth = len(p["g_mix"])
    for i in range(depth):
        if i % 2 == 0:
            qkv, o, gt = _proj_a(x, p["g_mix"][i], p["a_w_in"], p["a_w_gate_t"], p["a_b_gate"])
            hf, hb = _mlstm(qkv, gt)
            mixer, mixer_inputs, mixer_consts = "mlstm", (hf, hb, o), (p["a_norm_g"], p["a_w_out"])
        else:
            qt, k, vt = _proj_b(x, p["g_mix"][i], p["b_w_q_t"], p["b_w_k"], p["b_w_v_t"], t)
            o = _attn(qt, k, vt, bias, p["rel_bias"], p["b_lambda"], p["b_subln_g"], _lambda_init(i))
            mixer, mixer_inputs, mixer_consts = "attn", (o,), (p["b_w_out"],)
        kv = _mem_kv(mem, p["g_mem"][i], p["c_w_kv"][i])
        x = _tail(mixer, mixer_inputs, x, mixer_consts, p["g_cross"][i], p["c_w_q"][i], kv,
                  p["c_w_out"][i], p["g_mlp"][i], p["f_w1"][i], p["f_w2"][i], p["g_final"],
                  i == depth - 1)
    return x


def kernel(x_prompt, x_sample, mem_prompt, mem_sample, g_mix, g_cross, g_mem, g_mlp, g_final, a_w_in, a_w_gate, a_b_gate, a_norm_g, a_w_out, b_w_qkv, b_lambda, b_subln_g, b_w_out, rel_bias, c_w_q, c_w_kv, c_w_out, f_w1, f_w2):
    p = _prepare(g_mix, g_cross, g_mem, g_mlp, g_final, a_w_in, a_w_gate, a_b_gate, a_norm_g,
                 a_w_out, b_w_qkv, b_lambda, b_subln_g, b_w_out, rel_bias, c_w_q, c_w_kv,
                 c_w_out, f_w1, f_w2)
    t = min(TOKEN_TILE, x_prompt.shape[1], x_sample.shape[1])
    bias = _bias_tiles(p["rel_bias"], t)
    y_prompt = _encoder(x_prompt, mem_prompt, p, bias, t)
    y_sample = _encoder(x_sample, mem_sample, p, bias, t)
    return (y_prompt, y_sample)
```

```python
import functools
import math

import jax
import jax.numpy as jnp
from jax import lax
from jax.experimental import pallas as pl
from jax.experimental.pallas import tpu as pltpu

EPS = 1e-6
HA = 4
DK_A = 128
DV_A = 256
CHUNK = 128
MLSTM_GROUP = 8
HB = 8
DH_B = 64
N_BUCKETS = 32
MAX_DIST = 128
HC = 4

F32 = jnp.float32
BF16 = jnp.bfloat16

VMEM_LIMIT_BYTES = 56 * 1024 * 1024
TOKEN_TILE = 512
TAIL_TILE = 512
NT_DIMS = (((1,), (1,)), ((), ()))
LOG2E = math.log2(math.e)
NEAR_STEPS = 3
FAR_UNROLL_MAX = 14
NORM_MARGIN = 1.01
BOUND_SHIFT_MAX_SPREAD = 100.0


def _lambda_init(layer):
    return 0.8 - 0.6 * math.exp(-0.3 * layer)


def _params(*sem):
    return pltpu.CompilerParams(dimension_semantics=sem, vmem_limit_bytes=VMEM_LIMIT_BYTES)


def _resident(shape):
    zeros = (0,) * len(shape)
    return pl.BlockSpec(shape, lambda *_: zeros, pipeline_mode=pl.Buffered(1))


def _rms(x, g):
    return x * lax.rsqrt(jnp.mean(x * x, axis=-1, keepdims=True) + EPS) * g


def _dot(a, b):
    return jnp.dot(a, b, preferred_element_type=F32)


def _proj_a_kernel(x_ref, g_ref, w_ref, wgt_ref, bg_ref, qkv_ref, o_ref, gt_ref):
    xn = _rms(x_ref[...], g_ref[...]).astype(BF16)
    nqk = HA * DK_A
    for c in range(4):
        y = _dot(xn, w_ref[:, c * nqk:(c + 1) * nqk])
        if c == 1:
            y = y * (DK_A ** -0.5)
        qkv_ref[:, c * nqk:(c + 1) * nqk] = y.astype(BF16)
    for c in range(2):
        o_ref[:, c * nqk:(c + 1) * nqk] = _dot(xn, w_ref[:, (4 + c) * nqk:(5 + c) * nqk])
    gt = lax.dot_general(wgt_ref[...], xn, NT_DIMS, preferred_element_type=F32)
    gt_ref[...] = gt + bg_ref[...]


def _proj_a(x, g, w_in, w_gate_t, b_gate):
    B, S, D = x.shape
    ts = min(TOKEN_TILE, S)
    n_qkv = 2 * HA * DK_A + HA * DV_A
    n_o = HA * DV_A
    return pl.pallas_call(
        _proj_a_kernel,
        grid=(B, S // ts),
        in_specs=[
            pl.BlockSpec((None, ts, D), lambda b, i: (b, i, 0)),
            _resident((1, D)),
            _resident(w_in.shape),
            _resident(w_gate_t.shape),
            _resident(b_gate.shape),
        ],
        out_specs=[
            pl.BlockSpec((None, ts, n_qkv), lambda b, i: (b, i, 0)),
            pl.BlockSpec((None, ts, n_o), lambda b, i: (b, i, 0)),
            pl.BlockSpec((None, 4 * HA, ts), lambda b, i: (b, 0, i)),
        ],
        out_shape=[
            jax.ShapeDtypeStruct((B, S, n_qkv), BF16),
            jax.ShapeDtypeStruct((B, S, n_o), F32),
            jax.ShapeDtypeStruct((B, 4 * HA, S), F32),
        ],
        compiler_params=_params("parallel", "parallel"),
        name="proj_a",
    )(x, g, w_in, w_gate_t, b_gate)


def _log_sigmoid(x):
    return jnp.minimum(x, 0.0) - jnp.log1p(jnp.exp(-jnp.abs(x)))


def _mlstm_gate_prep(gf, gb):
    L = CHUNK
    row8 = lax.broadcasted_iota(jnp.int32, (2 * HA, L), 0)
    lane8 = lax.broadcasted_iota(jnp.int32, (2 * HA, L), 1)
    is_f = row8 < HA
    ig8 = jnp.where(is_f, gf, pltpu.roll(gb, HA, 0))
    fg8 = jnp.where(is_f, pltpu.roll(gf, HA, 0), gb)
    lf8 = _log_sigmoid(fg8)

    b8 = lf8
    d = 1
    while d < L:
        fwd = jnp.where(lane8 >= d, pltpu.roll(b8, d, 1), 0.0)
        bwd = jnp.where(lane8 < L - d, pltpu.roll(b8, L - d, 1), 0.0)
        b8 = b8 + jnp.where(is_f, fwd, bwd)
        d *= 2

    bl8 = jnp.where(is_f[:, 0:1], b8[:, L - 1:L], b8[:, 0:1])
    a8 = ig8 - b8
    wlog8 = bl8 + a8
    mw8 = jnp.max(wlog8, axis=1, keepdims=True)
    return b8, a8, jnp.exp(wlog8 - mw8), bl8, mw8


def _mlstm_kernel(qf_ref, kf_ref, vf_ref, qb_ref, kb_ref, vb_ref, gf_ref, gb_ref, gfn_ref, gbn_ref,
                  hf_ref, hb_ref, c_ref, m_ref, prep_ref, sc_ref):
    L = CHUNK
    nh2 = 2 * HA

    def store_prep(gf, gb):
        b8, a8, w8, bl8, mw8 = _mlstm_gate_prep(gf, gb)
        prep_ref[0] = b8
        prep_ref[1] = a8
        prep_ref[2] = w8
        sc_ref[0:nh2, :] = jnp.broadcast_to(bl8, (nh2, sc_ref.shape[1]))
        sc_ref[nh2:2 * nh2, :] = jnp.broadcast_to(mw8, (nh2, sc_ref.shape[1]))

    @pl.when(pl.program_id(1) == 0)
    def _():
        c_ref[...] = jnp.zeros(c_ref.shape, F32)
        m_ref[...] = jnp.zeros(m_ref.shape, F32)
        store_prep(gf_ref[...], gb_ref[...])

    b8 = prep_ref[0]
    a8 = prep_ref[1]
    w8 = prep_ref[2]
    bl8 = sc_ref[0:nh2, 0:1]
    mw8 = sc_ref[nh2:2 * nh2, 0:1]
    m_old8 = m_ref[:, 0:1]
    m_new8 = jnp.maximum(bl8 + m_old8, mw8)
    decay8 = jnp.exp(bl8 + m_old8 - m_new8)
    fw8 = jnp.exp(mw8 - m_new8)

    ti = lax.broadcasted_iota(jnp.int32, (L, L), 0)
    si = lax.broadcasted_iota(jnp.int32, (L, L), 1)

    ones = jnp.ones((L, 128), BF16)

    def lanes(x, reps):
        return jnp.concatenate([x] * reps, axis=1)

    def refs_of(r):
        if r < HA:
            return qf_ref, kf_ref, vf_ref, hf_ref, si <= ti
        return qb_ref, kb_ref, vb_ref, hb_ref, si >= ti

    for r0 in range(0, nh2, MLSTM_GROUP):
        group = range(r0, r0 + MLSTM_GROUP)
        st = {}
        for r in group:
            q_ref, k_ref, v_ref, out_ref, mask = refs_of(r)
            h = r % HA
            q = q_ref[:, h * DK_A:(h + 1) * DK_A]
            k = k_ref[:, h * DK_A:(h + 1) * DK_A]
            v1 = jnp.concatenate([v_ref[:, h * DV_A:(h + 1) * DV_A], ones], axis=1)
            b_mat = jnp.broadcast_to(b8[r:r + 1, :], (L, L)).T
            w_mat = jnp.broadcast_to(w8[r:r + 1, :], (L, L)).T
            dlog = jnp.where(mask, b_mat + a8[r:r + 1, :], -jnp.inf)
            st[r] = dict(q=q, k=k, v1=v1, w_mat=w_mat, dlog=dlog,
                         inter=b_mat + m_old8[r:r + 1, :],
                         qk=lax.dot_general(q, k, NT_DIMS, preferred_element_type=F32))
        for r in group:
            d = st[r]
            rmax = jnp.broadcast_to(jnp.max(d["dlog"], axis=1, keepdims=True), (L, L))
            d["m_t"] = jnp.maximum(d["inter"], rmax)
        for r in group:
            d = st[r]
            d["s"] = d["qk"] * jnp.exp(d["dlog"] - d["m_t"])
            d["iw"] = jnp.exp(d["inter"] - d["m_t"])
            d["c_old"] = c_ref[r]
        for r in group:
            d = st[r]
            _, _, _, out_ref, _ = refs_of(r)
            h = r % HA
            numden = (lanes(d["iw"], 3) * _dot(d["q"], d["c_old"].astype(BF16))
                      + _dot(d["s"].astype(BF16), d["v1"]))
            den = numden[:, DV_A:]
            inv = 1.0 / jnp.maximum(jnp.abs(den), jnp.exp(-d["m_t"]))
            out_ref[:, h * DV_A:(h + 1) * DV_A] = numden[:, :DV_A] * lanes(inv, 2)
        for r in group:
            d = st[r]
            wk = d["w_mat"] * d["k"].astype(F32)
            c_ref[r] = (decay8[r:r + 1, :] * d["c_old"]
                        + fw8[r:r + 1, :] * _dot(wk.T.astype(BF16), d["v1"]))

    m_ref[...] = jnp.broadcast_to(m_new8, m_ref.shape)
    store_prep(gfn_ref[...], gbn_ref[...])


def _mlstm(qkv, gt):
    B, S, _ = qkv.shape
    L = CHUNK
    nc = S // L
    nqk = HA * DK_A
    nv = HA * DV_A

    def fwd(col):
        return lambda b, c: (b, c, col)

    def bwd(col):
        return lambda b, c: (b, nc - 1 - c, col)

    return pl.pallas_call(
        _mlstm_kernel,
        grid=(B, nc),
        in_specs=[
            pl.BlockSpec((None, L, nqk), fwd(0)),
            pl.BlockSpec((None, L, nqk), fwd(1)),
            pl.BlockSpec((None, L, nv), fwd(1)),
            pl.BlockSpec((None, L, nqk), bwd(0)),
            pl.BlockSpec((None, L, nqk), bwd(1)),
            pl.BlockSpec((None, L, nv), bwd(1)),
            pl.BlockSpec((None, 2 * HA, L), lambda b, c: (b, 0, c)),
            pl.BlockSpec((None, 2 * HA, L), lambda b, c: (b, 1, nc - 1 - c)),
            pl.BlockSpec((None, 2 * HA, L), lambda b, c: (b, 0, jnp.minimum(c + 1, nc - 1))),
            pl.BlockSpec((None, 2 * HA, L), lambda b, c: (b, 1, jnp.maximum(nc - 2 - c, 0))),
        ],
        out_specs=[
            pl.BlockSpec((None, L, nv), fwd(0)),
            pl.BlockSpec((None, L, nv), bwd(0)),
        ],
        out_shape=[jax.ShapeDtypeStruct((B, S, nv), F32)] * 2,
        scratch_shapes=[
            pltpu.VMEM((2 * HA, DK_A, DV_A + 128), F32),
            pltpu.VMEM((2 * HA, 128), F32),
            pltpu.VMEM((3, 2 * HA, L), F32),
            pltpu.VMEM((4 * HA, 128), F32),
        ],
        compiler_params=_params("parallel", "arbitrary"),
        name="mlstm",
    )(qkv, qkv, qkv, qkv, qkv, qkv, gt, gt, gt, gt)


def _proj_b_kernel(x_ref, g_ref, wqt_ref, wk_ref, wvt_ref, qt_ref, k_ref, vt_ref):
    xn = _rms(x_ref[...], g_ref[...]).astype(BF16)
    qt = lax.dot_general(wqt_ref[...], xn, NT_DIMS, preferred_element_type=F32)
    qt_ref[...] = (qt * (DH_B ** -0.5 * LOG2E)).astype(BF16)
    vt = lax.dot_general(wvt_ref[...], xn, NT_DIMS, preferred_element_type=F32)
    vt_ref[...] = vt.astype(BF16)
    k_ref[...] = _dot(xn, wk_ref[...]).astype(BF16)


def _proj_b(x, g, w_q_t, w_k, w_v_t, t):
    B, S, D = x.shape
    nb = S // t
    n = w_k.shape[1]
    return pl.pallas_call(
        _proj_b_kernel,
        grid=(B, nb),
        in_specs=[
            pl.BlockSpec((None, t, D), lambda b, i: (b, i, 0)),
            _resident((1, D)),
            _resident(w_q_t.shape),
            _resident(w_k.shape),
            _resident(w_v_t.shape),
        ],
        out_specs=[
            pl.BlockSpec((None, None, n, t), lambda b, i: (b, i, 0, 0)),
            pl.BlockSpec((None, t, n), lambda b, i: (b, i, 0)),
            pl.BlockSpec((None, None, n, t), lambda b, i: (b, i, 0, 0)),
        ],
        out_shape=[
            jax.ShapeDtypeStruct((B, nb, n, t), BF16),
            jax.ShapeDtypeStruct((B, S, n), BF16),
            jax.ShapeDtypeStruct((B, nb, n, t), BF16),
        ],
        compiler_params=_params("parallel", "parallel"),
        name="proj_b",
    )(x, g, w_q_t, w_k, w_v_t)


def _bias_kernel(tab_ref, out_ref, *, t):
    h = pl.program_id(0)
    off = (pl.program_id(1) - 1) * t
    ki = lax.broadcasted_iota(jnp.int32, (t, t), 0)
    qi = lax.broadcasted_iota(jnp.int32, (t, t), 1)
    rp = off + ki - qi
    half = N_BUCKETS // 2
    max_exact = half // 2
    ret = (rp > 0).astype(jnp.int32) * half
    n = jnp.abs(rp)
    nf = jnp.maximum(n, 1).astype(F32)
    large = max_exact + (jnp.log(nf / max_exact) / math.log(MAX_DIST / max_exact)
                         * (half - max_exact)).astype(jnp.int32)
    large = jnp.minimum(large, half - 1)
    bucket = ret + jnp.where(n < max_exact, n, large)
    acc = jnp.zeros((t, t), F32)
    for b in range(N_BUCKETS):
        acc = jnp.where(bucket == b, tab_ref[b, h], acc)
    out_ref[...] = acc * LOG2E


def _bias_tiles(rel_bias, t):
    return pl.pallas_call(
        functools.partial(_bias_kernel, t=t),
        grid=(HB, 3),
        in_specs=[pl.BlockSpec(memory_space=pltpu.SMEM)],
        out_specs=pl.BlockSpec((None, None, t, t), lambda h, j: (h, j, 0, 0)),
        out_shape=jax.ShapeDtypeStruct((HB, 3, t, t), F32),
        compiler_params=_params("parallel", "parallel"),
        name="bias_tiles",
    )(rel_bias)


def _attn_kernel(qt_ref, k_ref, vt_ref, bias_ref, tab_ref, lam_ref, g_ref, o_ref,
                 q1_ref, q2_ref, m_ref, acc_ref, l8_ref, s_ref, p0_ref, p1_ref, kn_ref,
                 *, t, nk, lambda_init):
    h = pl.program_id(1)
    qi = pl.program_id(2)
    p_refs = (p0_ref, p1_ref)
    dv = 2 * DH_B

    @pl.when(qi == 0)
    def _():
        first_half = lax.broadcasted_iota(jnp.int32, (t, dv), 1) < DH_B

        def body(i, carry):
            kf = k_ref[pl.ds(pl.multiple_of(i * t, t), t), :].astype(F32)
            sq = kf * kf
            n1 = jnp.sum(jnp.where(first_half, sq, 0.0), axis=1, keepdims=True)
            n2 = jnp.sum(jnp.where(first_half, 0.0, sq), axis=1, keepdims=True)
            return (jnp.maximum(carry[0], jnp.max(n1, axis=0, keepdims=True)),
                    jnp.maximum(carry[1], jnp.max(n2, axis=0, keepdims=True)))

        zero = jnp.zeros((1, 1), F32)
        n1, n2 = lax.fori_loop(0, nk, body, (zero, zero))
        kn_ref[0:1, :] = jnp.broadcast_to(n1, (1, kn_ref.shape[1]))
        kn_ref[1:2, :] = jnp.broadcast_to(n2, (1, kn_ref.shape[1]))

    qt = qt_ref[...]
    rows = lax.broadcasted_iota(jnp.int32, qt.shape, 0)
    q1_ref[...] = jnp.where(rows < DH_B, qt, jnp.zeros_like(qt))
    q2_ref[...] = jnp.where(rows >= DH_B, qt, jnp.zeros_like(qt))
    acc_ref[...] = jnp.zeros(acc_ref.shape, F32)
    l8_ref[...] = jnp.zeros(l8_ref.shape, F32)

    c_left = tab_ref[N_BUCKETS // 2 - 1, h] * LOG2E
    c_right = tab_ref[N_BUCKETS - 1, h] * LOG2E
    b_max = tab_ref[0, h]
    b_min = tab_ref[0, h]
    for b in range(1, N_BUCKETS):
        b_max = jnp.maximum(b_max, tab_ref[b, h])
        b_min = jnp.minimum(b_min, tab_ref[b, h])
    b_max = b_max * LOG2E
    b_min = b_min * LOG2E

    def is_near(kj):
        return jnp.abs(kj - qi) <= 1

    def block_of(n):
        x = qi - 1 + n
        return jnp.where(x < 0, x + nk, jnp.where(x >= nk, x - nk, x))

    def key_block(kj):
        return k_ref[pl.ds(pl.multiple_of(kj * t, t), t), :]

    spread = jnp.zeros((1, t), F32)
    for j, q_ref in enumerate((q1_ref, q2_ref)):
        qf = q_ref[...].astype(F32)
        qn2 = jnp.sum(qf * qf, axis=0, keepdims=True)
        bound = jnp.sqrt(qn2 * kn_ref[j:j + 1, 0:1]) * NORM_MARGIN + 1.0
        m_ref[j:j + 1, :] = bound + b_max
        spread = jnp.maximum(spread, 2.0 * bound + (b_max - b_min))
    safe = jnp.max(spread) <= BOUND_SHIFT_MAX_SPREAD

    def bound_ab(n, par, maybe_near):
        kj = block_of(n)
        kb = key_block(kj)
        c = jnp.where(kj < qi, c_left, c_right)
        if maybe_near:
            near = is_near(kj)
            c = jnp.where(near, 0.0, c)
            w_bias = jnp.where(near, 1.0, 0.0)
            bt = bias_ref[jnp.clip(kj - qi + 1, 0, NEAR_STEPS - 1)] * w_bias
        for j, q_ref in enumerate((q1_ref, q2_ref)):
            s = _dot(kb, q_ref[...])
            if maybe_near:
                s = s + bt
            p = jnp.exp2(s - (m_ref[j:j + 1, :] - c))
            l8_ref[j] = l8_ref[j] + p.reshape(t // 8, 8, t).sum(axis=0)
            p_refs[par][j] = p.astype(BF16)

    def bound_c(n, par):
        vt = vt_ref[block_of(n)]
        for j in range(2):
            acc_ref[j] = acc_ref[j] + _dot(vt, p_refs[par][j])

    @pl.when(safe)
    def _():
        def steps(n0, count, maybe_near):
            for u in range(count):
                par = (n0 + u) % 2 if isinstance(n0, int) else (NEAR_STEPS + u) % 2
                bound_ab(n0 + u + 1, 1 - par, maybe_near)
                bound_c(n0 + u, par)

        n_far = nk - 1 - NEAR_STEPS
        unroll = max(u for u in range(2, FAR_UNROLL_MAX + 1, 2) if n_far % u == 0)

        def body(i, carry):
            steps(NEAR_STEPS + unroll * i, unroll, False)
            return carry

        bound_ab(0, 0, True)
        steps(0, NEAR_STEPS - 1, True)
        steps(NEAR_STEPS - 1, 1, False)
        lax.fori_loop(0, n_far // unroll, body, 0)
        bound_c(nk - 1, (nk - 1) % 2)

    @pl.when(jnp.logical_not(safe))
    def _():
        m_ref[...] = jnp.full(m_ref.shape, -jnp.inf, F32)

        def body(n, carry):
            kj = block_of(n)
            kb = key_block(kj)
            near = is_near(kj)
            c = jnp.where(near, 0.0, jnp.where(kj < qi, c_left, c_right))
            for j, q_ref in enumerate((q1_ref, q2_ref)):
                s_ref[...] = _dot(kb, q_ref[...])

                @pl.when(near)
                def _():
                    s_ref[...] = s_ref[...] + bias_ref[jnp.clip(kj - qi + 1, 0, NEAR_STEPS - 1)]

                s = s_ref[...]
                m_old = m_ref[j:j + 1, :]
                m_new = jnp.maximum(m_old, jnp.max(s, axis=0, keepdims=True) + c)
                p = jnp.exp2(s - (m_new - c))
                alpha = jnp.exp2(m_old - m_new)
                l8_ref[j] = alpha * l8_ref[j] + p.reshape(t // 8, 8, t).sum(axis=0)
                acc_ref[j] = alpha * acc_ref[j] + _dot(vt_ref[kj], p.astype(BF16))
                m_ref[j:j + 1, :] = m_new
            return carry

        lax.fori_loop(0, nk, body, 0)

    lv = lam_ref[...]
    lam = (jnp.exp(jnp.sum(lv[0:1] * lv[1:2], axis=1, keepdims=True))
           - jnp.exp(jnp.sum(lv[2:3] * lv[3:4], axis=1, keepdims=True)) + lambda_init)
    l1 = jnp.sum(l8_ref[0], axis=0, keepdims=True)
    l2 = jnp.sum(l8_ref[1], axis=0, keepdims=True)
    o = acc_ref[0] * (1.0 / l1) - lam * (acc_ref[1] * (1.0 / l2))
    on = o * lax.rsqrt(jnp.mean(o * o, axis=0, keepdims=True) + EPS) * g_ref[...]
    on = on * (1.0 - lambda_init)
    o_ref[...] = on.T.astype(BF16)


def _attn(qt, k, vt, bias, rel_bias, lam_vecs, subln_g, lambda_init):
    B, nb, n, t = qt.shape
    S = k.shape[1]
    assert nb % 2 == 0 and nb > NEAR_STEPS, nb
    assert t >= MAX_DIST
    dh2 = 2 * DH_B
    return pl.pallas_call(
        functools.partial(_attn_kernel, t=t, nk=nb, lambda_init=lambda_init),
        grid=(B, HB, nb),
        in_specs=[
            pl.BlockSpec((None, None, dh2, t), lambda b, h, i: (b, i, h, 0)),
            pl.BlockSpec((None, S, dh2), lambda b, h, i: (b, 0, h)),
            pl.BlockSpec((None, nb, dh2, t), lambda b, h, i: (b, 0, h, 0)),
            pl.BlockSpec((None, NEAR_STEPS, t, t), lambda b, h, i: (h, 0, 0, 0)),
            pl.BlockSpec(memory_space=pltpu.SMEM),
            _resident(lam_vecs.shape),
            _resident(subln_g.shape),
        ],
        out_specs=pl.BlockSpec((None, t, dh2), lambda b, h, i: (b, i, h)),
        out_shape=jax.ShapeDtypeStruct((B, S, n), BF16),
        scratch_shapes=[
            pltpu.VMEM((dh2, t), BF16),
            pltpu.VMEM((dh2, t), BF16),
            pltpu.VMEM((2, t), F32),
            pltpu.VMEM((2, dh2, t), F32),
            pltpu.VMEM((2, 8, t), F32),
            pltpu.VMEM((t, t), F32),
            pltpu.VMEM((2, t, t), BF16),
            pltpu.VMEM((2, t, t), BF16),
            pltpu.VMEM((2, 128), F32),
        ],
        compiler_params=_params("parallel", "parallel", "arbitrary"),
        name="diff_attn",
    )(qt, k, vt, bias, rel_bias, lam_vecs, subln_g)


def _mem_kv_kernel(mem_ref, g_ref, w_ref, kv_ref):
    mn = _rms(mem_ref[...], g_ref[...]).astype(BF16)
    kv_ref[...] = _dot(mn, w_ref[...]).astype(BF16)


def _mem_kv(mem, g, w_kv):
    B, M, D = mem.shape
    n = w_kv.shape[1]
    return pl.pallas_call(
        _mem_kv_kernel,
        grid=(B,),
        in_specs=[pl.BlockSpec((None, M, D), lambda b: (b, 0, 0)), _resident((1, D)),
                  _resident(w_kv.shape)],
        out_specs=pl.BlockSpec((None, M, n), lambda b: (b, 0, 0)),
        out_shape=jax.ShapeDtypeStruct((B, M, n), BF16),
        compiler_params=_params("parallel"),
        name="mem_kv",
    )(mem, g, w_kv)


def _tail_kernel(*refs, mixer, final_norm):
    if mixer == "mlstm":
        (hf_ref, hb_ref, o_ref, x_ref, ng_ref, wm_ref, gc_ref, wq_ref, kv_ref, wo_ref,
         gm_ref, w1_ref, w2_ref, gf_ref, out_ref) = refs
        parts = []
        for h in range(HA):
            sl = slice(h * DV_A, (h + 1) * DV_A)
            hh = hf_ref[:, sl] + hb_ref[:, sl]
            hn = hh * lax.rsqrt(jnp.mean(hh * hh, axis=-1, keepdims=True) + EPS)
            hn = hn * ng_ref[:, sl] * jax.nn.sigmoid(o_ref[:, sl])
            parts.append(hn.astype(BF16))
        mixed = jnp.concatenate(parts, axis=1)
    else:
        (o_ref, x_ref, wm_ref, gc_ref, wq_ref, kv_ref, wo_ref,
         gm_ref, w1_ref, w2_ref, gf_ref, out_ref) = refs
        mixed = o_ref[...]
    x = x_ref[...] + _dot(mixed, wm_ref[...])

    D = x.shape[-1]
    dc = D // HC
    xn = _rms(x, gc_ref[...]).astype(BF16)
    q = _dot(xn, wq_ref[...])
    parts = []
    for h in range(HC):
        qh = (q[:, h * dc:(h + 1) * dc] * (dc ** -0.5)).astype(BF16)
        kh = kv_ref[:, h * dc:(h + 1) * dc]
        vh = kv_ref[:, D + h * dc:D + (h + 1) * dc]
        s = lax.dot_general(qh, kh, NT_DIMS, preferred_element_type=F32)
        p = jnp.exp(s - jnp.max(s, axis=-1, keepdims=True))
        inv = 1.0 / jnp.sum(p, axis=-1, keepdims=True)
        parts.append((_dot(p.astype(BF16), vh) * inv).astype(BF16))
    x = x + _dot(jnp.concatenate(parts, axis=1), wo_ref[...])

    xn = _rms(x, gm_ref[...]).astype(BF16)
    acc = x
    for c in range(w1_ref.shape[1] // D):
        hcol = _dot(xn, w1_ref[:, c * D:(c + 1) * D])
        hcol = jnp.square(jnp.maximum(hcol, 0.0)).astype(BF16)
        acc = acc + _dot(hcol, w2_ref[c * D:(c + 1) * D, :])
    out_ref[...] = _rms(acc, gf_ref[...]) if final_norm else acc


def _tail(mixer, mixer_inputs, x, mixer_consts, g_cross, w_q, kv, w_out, g_mlp, w1, w2, g_final,
          final_norm):
    B, S, D = x.shape
    M = kv.shape[1]
    ts = min(TAIL_TILE, S)
    tile = lambda n: pl.BlockSpec((None, ts, n), lambda b, i: (b, i, 0))
    in_specs = ([tile(a.shape[-1]) for a in mixer_inputs] + [tile(D)]
                + [_resident(c.shape) for c in mixer_consts]
                + [_resident((1, D)), _resident(w_q.shape),
                   pl.BlockSpec((None, M, 2 * D), lambda b, i: (b, 0, 0)),
                   _resident(w_out.shape), _resident((1, D)), _resident(w1.shape),
                   _resident(w2.shape), _resident((1, D))])
    return pl.pallas_call(
        functools.partial(_tail_kernel, mixer=mixer, final_norm=final_norm),
        grid=(B, S // ts),
        in_specs=in_specs,
        out_specs=tile(D),
        out_shape=jax.ShapeDtypeStruct((B, S, D), F32),
        compiler_params=_params("parallel", "parallel"),
        name="tail_" + mixer,
    )(*mixer_inputs, x, *mixer_consts, g_cross, w_q, kv, w_out, g_mlp, w1, w2, g_final)


def _prepare(g_mix, g_cross, g_mem, g_mlp, g_final, a_w_in, a_w_gate, a_b_gate, a_norm_g,
             a_w_out, b_w_qkv, b_lambda, b_subln_g, b_w_out, rel_bias, c_w_q, c_w_kv,
             c_w_out, f_w1, f_w2):
    row = lambda v: v.reshape(1, -1).astype(F32)
    nq = 2 * HB * DH_B
    wqkv = b_w_qkv[0]
    return dict(
        g_mix=[row(g) for g in g_mix], g_cross=[row(g) for g in g_cross],
        g_mem=[row(g) for g in g_mem], g_mlp=[row(g) for g in g_mlp], g_final=row(g_final),
        a_w_in=a_w_in[0].astype(BF16), a_w_gate_t=a_w_gate[0].T.astype(BF16),
        a_b_gate=a_b_gate[0].reshape(-1, 1).astype(F32), a_norm_g=row(a_norm_g[0]),
        a_w_out=a_w_out[0].astype(BF16),
        b_w_q_t=wqkv[:, :nq].T.astype(BF16), b_w_k=wqkv[:, nq:2 * nq].astype(BF16),
        b_w_v_t=wqkv[:, 2 * nq:].T.astype(BF16),
        b_lambda=b_lambda[0].astype(F32), b_subln_g=b_subln_g[0].reshape(-1, 1).astype(F32),
        b_w_out=b_w_out[0].astype(BF16),
        rel_bias=rel_bias.astype(F32),
        c_w_q=[w.astype(BF16) for w in c_w_q], c_w_kv=[w.astype(BF16) for w in c_w_kv],
        c_w_out=[w.astype(BF16) for w in c_w_out],
        f_w1=[w.astype(BF16) for w in f_w1], f_w2=[w.astype(BF16) for w in f_w2],
    )


def _encoder(x, mem, p, bias, t):
    depth = len(p["g_mix"])
    for i in range(depth):
        if i % 2 == 0:
            qkv, o, gt = _proj_a(x, p["g_mix"][i], p["a_w_in"], p["a_w_gate_t"], p["a_b_gate"])
            hf, hb = _mlstm(qkv, gt)
            mixer, mixer_inputs, mixer_consts = "mlstm", (hf, hb, o), (p["a_norm_g"], p["a_w_out"])
        else:
            qt, k, vt = _proj_b(x, p["g_mix"][i], p["b_w_q_t"], p["b_w_k"], p["b_w_v_t"], t)
            o = _attn(qt, k, vt, bias, p["rel_bias"], p["b_lambda"], p["b_subln_g"], _lambda_init(i))
            mixer, mixer_inputs, mixer_consts = "attn", (o,), (p["b_w_out"],)
        kv = _mem_kv(mem, p["g_mem"][i], p["c_w_kv"][i])
        x = _tail(mixer, mixer_inputs, x, mixer_consts, p["g_cross"][i], p["c_w_q"][i], kv,
                  p["c_w_out"][i], p["g_mlp"][i], p["f_w1"][i], p["f_w2"][i], p["g_final"],
                  i == depth - 1)
    return x


def kernel(x_prompt, x_sample, mem_prompt, mem_sample, g_mix, g_cross, g_mem, g_mlp, g_final, a_w_in, a_w_gate, a_b_gate, a_norm_g, a_w_out, b_w_qkv, b_lambda, b_subln_g, b_w_out, rel_bias, c_w_q, c_w_kv, c_w_out, f_w1, f_w2):
    p = _prepare(g_mix, g_cross, g_mem, g_mlp, g_final, a_w_in, a_w_gate, a_b_gate, a_norm_g,
                 a_w_out, b_w_qkv, b_lambda, b_subln_g, b_w_out, rel_bias, c_w_q, c_w_kv,
                 c_w_out, f_w1, f_w2)
    t = min(TOKEN_TILE, x_prompt.shape[1], x_sample.shape[1])
    bias = _bias_tiles(p["rel_bias"], t)
    y_prompt = _encoder(x_prompt, mem_prompt, p, bias, t)
    y_sample = _encoder(x_sample, mem_sample, p, bias, t)
    return (y_prompt, y_sample)
```

```python
import functools
import math

import jax
import jax.numpy as jnp
from jax import lax
from jax.experimental import pallas as pl
from jax.experimental.pallas import tpu as pltpu

EPS = 1e-6
HA = 4
DK_A = 128
DV_A = 256
CHUNK = 128
MLSTM_ROWS = 2
MLSTM_GROUP = 16
HB = 8
DH_B = 64
N_BUCKETS = 32
MAX_DIST = 128
HC = 4

F32 = jnp.float32
BF16 = jnp.bfloat16

VMEM_LIMIT_BYTES = 56 * 1024 * 1024
TOKEN_TILE = 512
TAIL_TILE = 512
NT_DIMS = (((1,), (1,)), ((), ()))
LOG2E = math.log2(math.e)
NEAR_STEPS = 3
FAR_UNROLL_MAX = 14
NORM_MARGIN = 1.01
BOUND_SHIFT_MAX_SPREAD = 100.0


def _lambda_init(layer):
    return 0.8 - 0.6 * math.exp(-0.3 * layer)


def _params(*sem):
    return pltpu.CompilerParams(dimension_semantics=sem, vmem_limit_bytes=VMEM_LIMIT_BYTES)


def _resident(shape):
    zeros = (0,) * len(shape)
    return pl.BlockSpec(shape, lambda *_: zeros, pipeline_mode=pl.Buffered(1))


def _rms(x, g):
    return x * lax.rsqrt(jnp.mean(x * x, axis=-1, keepdims=True) + EPS) * g


def _dot(a, b):
    return jnp.dot(a, b, preferred_element_type=F32)


def _proj_a_kernel(x_ref, g_ref, w_ref, wgt_ref, bg_ref, qkv_ref, o_ref, gt_ref):
    xn = _rms(x_ref[...], g_ref[...]).astype(BF16)
    nqk = HA * DK_A
    for c in range(4):
        y = _dot(xn, w_ref[:, c * nqk:(c + 1) * nqk])
        if c == 1:
            y = y * (DK_A ** -0.5)
        qkv_ref[:, c * nqk:(c + 1) * nqk] = y.astype(BF16)
    for c in range(2):
        o_ref[:, c * nqk:(c + 1) * nqk] = _dot(xn, w_ref[:, (4 + c) * nqk:(5 + c) * nqk])
    gt = lax.dot_general(wgt_ref[...], xn, NT_DIMS, preferred_element_type=F32)
    gt_ref[...] = gt + bg_ref[...]


def _proj_a(x, g, w_in, w_gate_t, b_gate):
    B, S, D = x.shape
    ts = min(TOKEN_TILE, S)
    n_qkv = 2 * HA * DK_A + HA * DV_A
    n_o = HA * DV_A
    return pl.pallas_call(
        _proj_a_kernel,
        grid=(B, S // ts),
        in_specs=[
            pl.BlockSpec((None, ts, D), lambda b, i: (b, i, 0)),
            _resident((1, D)),
            _resident(w_in.shape),
            _resident(w_gate_t.shape),
            _resident(b_gate.shape),
        ],
        out_specs=[
            pl.BlockSpec((None, ts, n_qkv), lambda b, i: (b, i, 0)),
            pl.BlockSpec((None, ts, n_o), lambda b, i: (b, i, 0)),
            pl.BlockSpec((None, 4 * HA, ts), lambda b, i: (b, 0, i)),
        ],
        out_shape=[
            jax.ShapeDtypeStruct((B, S, n_qkv), BF16),
            jax.ShapeDtypeStruct((B, S, n_o), F32),
            jax.ShapeDtypeStruct((B, 4 * HA, S), F32),
        ],
        compiler_params=_params("parallel", "parallel"),
        name="proj_a",
    )(x, g, w_in, w_gate_t, b_gate)


def _log_sigmoid(x):
    return jnp.minimum(x, 0.0) - jnp.log1p(jnp.exp(-jnp.abs(x)))


def _mlstm_gate_prep(gf, gb):
    L = CHUNK
    row8 = lax.broadcasted_iota(jnp.int32, (2 * HA, L), 0)
    lane8 = lax.broadcasted_iota(jnp.int32, (2 * HA, L), 1)
    is_f = row8 < HA
    ig8 = jnp.where(is_f, gf, pltpu.roll(gb, HA, 0))
    fg8 = jnp.where(is_f, pltpu.roll(gf, HA, 0), gb)
    lf8 = _log_sigmoid(fg8)

    b8 = lf8
    d = 1
    while d < L:
        fwd = jnp.where(lane8 >= d, pltpu.roll(b8, d, 1), 0.0)
        bwd = jnp.where(lane8 < L - d, pltpu.roll(b8, L - d, 1), 0.0)
        b8 = b8 + jnp.where(is_f, fwd, bwd)
        d *= 2

    bl8 = jnp.where(is_f[:, 0:1], b8[:, L - 1:L], b8[:, 0:1])
    a8 = ig8 - b8
    wlog8 = bl8 + a8
    mw8 = jnp.max(wlog8, axis=1, keepdims=True)
    return b8, a8, jnp.exp(wlog8 - mw8), bl8, mw8


def _mlstm_kernel(qf_ref, kf_ref, vf_ref, qb_ref, kb_ref, vb_ref, gf_ref, gb_ref, gfn_ref, gbn_ref,
                  hf_ref, hb_ref, c_ref, m_ref, prep_ref, sc_ref):
    L = CHUNK
    nh2 = 2 * HA
    nrows = gf_ref.shape[0]

    def store_prep(g_f, g_b):
        for bi in range(nrows):
            b8, a8, w8, bl8, mw8 = _mlstm_gate_prep(g_f[bi], g_b[bi])
            prep_ref[bi, 0] = b8
            prep_ref[bi, 1] = a8
            prep_ref[bi, 2] = w8
            sc_ref[bi, 0:nh2, :] = jnp.broadcast_to(bl8, (nh2, sc_ref.shape[2]))
            sc_ref[bi, nh2:2 * nh2, :] = jnp.broadcast_to(mw8, (nh2, sc_ref.shape[2]))

    @pl.when(pl.program_id(1) == 0)
    def _():
        c_ref[...] = jnp.zeros(c_ref.shape, F32)
        m_ref[...] = jnp.zeros(m_ref.shape, F32)
        store_prep(gf_ref, gb_ref)

    ti = lax.broadcasted_iota(jnp.int32, (L, L), 0)
    si = lax.broadcasted_iota(jnp.int32, (L, L), 1)
    ones = jnp.ones((L, 128), BF16)

    def lanes(x, reps):
        return jnp.concatenate([x] * reps, axis=1)

    def refs_of(r):
        if r < HA:
            return qf_ref, kf_ref, vf_ref, hf_ref, si <= ti
        return qb_ref, kb_ref, vb_ref, hb_ref, si >= ti

    gates = []
    for bi in range(nrows):
        bl8 = sc_ref[bi, 0:nh2, 0:1]
        mw8 = sc_ref[bi, nh2:2 * nh2, 0:1]
        m_old8 = m_ref[bi, :, 0:1]
        m_new8 = jnp.maximum(bl8 + m_old8, mw8)
        gates.append(dict(b8=prep_ref[bi, 0], a8=prep_ref[bi, 1],
                          w8=prep_ref[bi, 2],
                          m_old8=m_old8, m_new8=m_new8,
                          decay8=jnp.exp(bl8 + m_old8 - m_new8), fw8=jnp.exp(mw8 - m_new8)))

    chains = [(bi, r) for r in range(nh2) for bi in range(nrows)]
    for g0 in range(0, len(chains), MLSTM_GROUP):
        group = chains[g0:g0 + MLSTM_GROUP]
        st = {}
        for bi, r in group:
            q_ref, k_ref, v_ref, out_ref, mask = refs_of(r)
            g = gates[bi]
            h = r % HA
            q = q_ref[bi, :, h * DK_A:(h + 1) * DK_A]
            k = k_ref[bi, :, h * DK_A:(h + 1) * DK_A]
            v1 = jnp.concatenate([v_ref[bi, :, h * DV_A:(h + 1) * DV_A], ones], axis=1)
            b_mat = jnp.broadcast_to(g["b8"][r:r + 1, :], (L, L)).T
            w_mat = jnp.broadcast_to(g["w8"][r:r + 1, :], (L, L)).T
            dlog = jnp.where(mask, b_mat + g["a8"][r:r + 1, :], -jnp.inf)
            st[bi, r] = dict(q=q, k=k, v1=v1, w_mat=w_mat, dlog=dlog,
                             inter=b_mat + g["m_old8"][r:r + 1, :],
                             qk=lax.dot_general(q, k, NT_DIMS, preferred_element_type=F32))
        for key in group:
            d = st[key]
            rmax = jnp.broadcast_to(jnp.max(d["dlog"], axis=1, keepdims=True), (L, L))
            d["m_t"] = jnp.maximum(d["inter"], rmax)
        for bi, r in group:
            d = st[bi, r]
            d["s"] = d["qk"] * jnp.exp(d["dlog"] - d["m_t"])
            d["iw"] = jnp.exp(d["inter"] - d["m_t"])
            d["c_old"] = c_ref[bi, r]
        for bi, r in group:
            d = st[bi, r]
            _, _, _, out_ref, _ = refs_of(r)
            h = r % HA
            numden = (lanes(d["iw"], 3) * _dot(d["q"], d["c_old"].astype(BF16))
                      + _dot(d["s"].astype(BF16), d["v1"]))
            den = numden[:, DV_A:]
            inv = 1.0 / jnp.maximum(jnp.abs(den), jnp.exp(-d["m_t"]))
            out_ref[bi, :, h * DV_A:(h + 1) * DV_A] = numden[:, :DV_A] * lanes(inv, 2)
        for bi, r in group:
            d = st[bi, r]
            g = gates[bi]
            wk = d["w_mat"] * d["k"].astype(F32)
            c_ref[bi, r] = (g["decay8"][r:r + 1, :] * d["c_old"]
                            + g["fw8"][r:r + 1, :] * _dot(wk.T.astype(BF16), d["v1"]))

    for bi in range(nrows):
        m_ref[bi] = jnp.broadcast_to(gates[bi]["m_new8"], m_ref.shape[1:])
    store_prep(gfn_ref, gbn_ref)


def _mlstm(qkv, gt):
    B, S, _ = qkv.shape
    L = CHUNK
    nc = S // L
    nqk = HA * DK_A
    nv = HA * DV_A

    def fwd(col):
        return lambda b, c: (b, c, col)

    def bwd(col):
        return lambda b, c: (b, nc - 1 - c, col)

    nr = MLSTM_ROWS
    assert B % nr == 0, B
    gates = lambda index_map: pl.BlockSpec((nr, 2 * HA, L), index_map)
    return pl.pallas_call(
        _mlstm_kernel,
        grid=(B // nr, nc),
        in_specs=[
            pl.BlockSpec((nr, L, nqk), fwd(0)),
            pl.BlockSpec((nr, L, nqk), fwd(1)),
            pl.BlockSpec((nr, L, nv), fwd(1)),
            pl.BlockSpec((nr, L, nqk), bwd(0)),
            pl.BlockSpec((nr, L, nqk), bwd(1)),
            pl.BlockSpec((nr, L, nv), bwd(1)),
            gates(lambda b, c: (b, 0, c)),
            gates(lambda b, c: (b, 1, nc - 1 - c)),
            gates(lambda b, c: (b, 0, jnp.minimum(c + 1, nc - 1))),
            gates(lambda b, c: (b, 1, jnp.maximum(nc - 2 - c, 0))),
        ],
        out_specs=[
            pl.BlockSpec((nr, L, nv), fwd(0)),
            pl.BlockSpec((nr, L, nv), bwd(0)),
        ],
        out_shape=[jax.ShapeDtypeStruct((B, S, nv), F32)] * 2,
        scratch_shapes=[
            pltpu.VMEM((nr, 2 * HA, DK_A, DV_A + 128), F32),
            pltpu.VMEM((nr, 2 * HA, 128), F32),
            pltpu.VMEM((nr, 3, 2 * HA, L), F32),
            pltpu.VMEM((nr, 4 * HA, 128), F32),
        ],
        compiler_params=_params("parallel", "arbitrary"),
        name="mlstm",
    )(qkv, qkv, qkv, qkv, qkv, qkv, gt, gt, gt, gt)


def _proj_b_kernel(x_ref, g_ref, wqt_ref, wk_ref, wvt_ref, qt_ref, k_ref, vt_ref):
    xn = _rms(x_ref[...], g_ref[...]).astype(BF16)
    qt = lax.dot_general(wqt_ref[...], xn, NT_DIMS, preferred_element_type=F32)
    qt_ref[...] = (qt * (DH_B ** -0.5 * LOG2E)).astype(BF16)
    vt = lax.dot_general(wvt_ref[...], xn, NT_DIMS, preferred_element_type=F32)
    vt_ref[...] = vt.astype(BF16)
    k_ref[...] = _dot(xn, wk_ref[...]).astype(BF16)


def _proj_b(x, g, w_q_t, w_k, w_v_t, t):
    B, S, D = x.shape
    nb = S // t
    n = w_k.shape[1]
    return pl.pallas_call(
        _proj_b_kernel,
        grid=(B, nb),
        in_specs=[
            pl.BlockSpec((None, t, D), lambda b, i: (b, i, 0)),
            _resident((1, D)),
            _resident(w_q_t.shape),
            _resident(w_k.shape),
            _resident(w_v_t.shape),
        ],
        out_specs=[
            pl.BlockSpec((None, None, n, t), lambda b, i: (b, i, 0, 0)),
            pl.BlockSpec((None, t, n), lambda b, i: (b, i, 0)),
            pl.BlockSpec((None, None, n, t), lambda b, i: (b, i, 0, 0)),
        ],
        out_shape=[
            jax.ShapeDtypeStruct((B, nb, n, t), BF16),
            jax.ShapeDtypeStruct((B, S, n), BF16),
            jax.ShapeDtypeStruct((B, nb, n, t), BF16),
        ],
        compiler_params=_params("parallel", "parallel"),
        name="proj_b",
    )(x, g, w_q_t, w_k, w_v_t)


def _bias_kernel(tab_ref, out_ref, *, t):
    h = pl.program_id(0)
    off = (pl.program_id(1) - 1) * t
    ki = lax.broadcasted_iota(jnp.int32, (t, t), 0)
    qi = lax.broadcasted_iota(jnp.int32, (t, t), 1)
    rp = off + ki - qi
    half = N_BUCKETS // 2
    max_exact = half // 2
    ret = (rp > 0).astype(jnp.int32) * half
    n = jnp.abs(rp)
    nf = jnp.maximum(n, 1).astype(F32)
    large = max_exact + (jnp.log(nf / max_exact) / math.log(MAX_DIST / max_exact)
                         * (half - max_exact)).astype(jnp.int32)
    large = jnp.minimum(large, half - 1)
    bucket = ret + jnp.where(n < max_exact, n, large)
    acc = jnp.zeros((t, t), F32)
    for b in range(N_BUCKETS):
        acc = jnp.where(bucket == b, tab_ref[b, h], acc)
    out_ref[...] = acc * LOG2E


def _bias_tiles(rel_bias, t):
    return pl.pallas_call(
        functools.partial(_bias_kernel, t=t),
        grid=(HB, 3),
        in_specs=[pl.BlockSpec(memory_space=pltpu.SMEM)],
        out_specs=pl.BlockSpec((None, None, t, t), lambda h, j: (h, j, 0, 0)),
        out_shape=jax.ShapeDtypeStruct((HB, 3, t, t), F32),
        compiler_params=_params("parallel", "parallel"),
        name="bias_tiles",
    )(rel_bias)


def _attn_kernel(qt_ref, k_ref, vt_ref, bias_ref, tab_ref, lam_ref, g_ref, o_ref,
                 q1_ref, q2_ref, m_ref, acc_ref, l8_ref, s_ref, p0_ref, p1_ref, kn_ref,
                 *, t, nk, lambda_init):
    h = pl.program_id(1)
    qi = pl.program_id(2)
    p_refs = (p0_ref, p1_ref)
    dv = 2 * DH_B

    @pl.when(qi == 0)
    def _():
        first_half = lax.broadcasted_iota(jnp.int32, (t, dv), 1) < DH_B

        def body(i, carry):
            kf = k_ref[pl.ds(pl.multiple_of(i * t, t), t), :].astype(F32)
            sq = kf * kf
            n1 = jnp.sum(jnp.where(first_half, sq, 0.0), axis=1, keepdims=True)
            n2 = jnp.sum(jnp.where(first_half, 0.0, sq), axis=1, keepdims=True)
            return (jnp.maximum(carry[0], jnp.max(n1, axis=0, keepdims=True)),
                    jnp.maximum(carry[1], jnp.max(n2, axis=0, keepdims=True)))

        zero = jnp.zeros((1, 1), F32)
        n1, n2 = lax.fori_loop(0, nk, body, (zero, zero))
        kn_ref[0:1, :] = jnp.broadcast_to(n1, (1, kn_ref.shape[1]))
        kn_ref[1:2, :] = jnp.broadcast_to(n2, (1, kn_ref.shape[1]))

    qt = qt_ref[...]
    rows = lax.broadcasted_iota(jnp.int32, qt.shape, 0)
    q1_ref[...] = jnp.where(rows < DH_B, qt, jnp.zeros_like(qt))
    q2_ref[...] = jnp.where(rows >= DH_B, qt, jnp.zeros_like(qt))
    acc_ref[...] = jnp.zeros(acc_ref.shape, F32)
    l8_ref[...] = jnp.zeros(l8_ref.shape, F32)

    c_left = tab_ref[N_BUCKETS // 2 - 1, h] * LOG2E
    c_right = tab_ref[N_BUCKETS - 1, h] * LOG2E
    b_max = tab_ref[0, h]
    b_min = tab_ref[0, h]
    for b in range(1, N_BUCKETS):
        b_max = jnp.maximum(b_max, tab_ref[b, h])
        b_min = jnp.minimum(b_min, tab_ref[b, h])
    b_max = b_max * LOG2E
    b_min = b_min * LOG2E

    def is_near(kj):
        return jnp.abs(kj - qi) <= 1

    def block_of(n):
        x = qi - 1 + n
        return jnp.where(x < 0, x + nk, jnp.where(x >= nk, x - nk, x))

    def key_block(kj):
        return k_ref[pl.ds(pl.multiple_of(kj * t, t), t), :]

    spread = jnp.zeros((1, t), F32)
    for j, q_ref in enumerate((q1_ref, q2_ref)):
        qf = q_ref[...].astype(F32)
        qn2 = jnp.sum(qf * qf, axis=0, keepdims=True)
        bound = jnp.sqrt(qn2 * kn_ref[j:j + 1, 0:1]) * NORM_MARGIN + 1.0
        m_ref[j:j + 1, :] = bound + b_max
        spread = jnp.maximum(spread, 2.0 * bound + (b_max - b_min))
    safe = jnp.max(spread) <= BOUND_SHIFT_MAX_SPREAD

    def bound_ab(n, par, maybe_near):
        kj = block_of(n)
        kb = key_block(kj)
        c = jnp.where(kj < qi, c_left, c_right)
        if maybe_near:
            near = is_near(kj)
            c = jnp.where(near, 0.0, c)
            w_bias = jnp.where(near, 1.0, 0.0)
            bt = bias_ref[jnp.clip(kj - qi + 1, 0, NEAR_STEPS - 1)] * w_bias
        for j, q_ref in enumerate((q1_ref, q2_ref)):
            s = _dot(kb, q_ref[...])
            if maybe_near:
                s = s + bt
            p = jnp.exp2(s - (m_ref[j:j + 1, :] - c))
            l8_ref[j] = l8_ref[j] + p.reshape(t // 8, 8, t).sum(axis=0)
            p_refs[par][j] = p.astype(BF16)

    def bound_c(n, par):
        vt = vt_ref[block_of(n)]
        for j in range(2):
            acc_ref[j] = acc_ref[j] + _dot(vt, p_refs[par][j])

    @pl.when(safe)
    def _():
        def steps(n0, count, maybe_near):
            for u in range(count):
                par = (n0 + u) % 2 if isinstance(n0, int) else (NEAR_STEPS + u) % 2
                bound_ab(n0 + u + 1, 1 - par, maybe_near)
                bound_c(n0 + u, par)

        n_far = nk - 1 - NEAR_STEPS
        unroll = max(u for u in range(2, FAR_UNROLL_MAX + 1, 2) if n_far % u == 0)

        def body(i, carry):
            steps(NEAR_STEPS + unroll * i, unroll, False)
            return carry

        bound_ab(0, 0, True)
        steps(0, NEAR_STEPS - 1, True)
        steps(NEAR_STEPS - 1, 1, False)
        lax.fori_loop(0, n_far // unroll, body, 0)
        bound_c(nk - 1, (nk - 1) % 2)

    @pl.when(jnp.logical_not(safe))
    def _():
        m_ref[...] = jnp.full(m_ref.shape, -jnp.inf, F32)

        def body(n, carry):
            kj = block_of(n)
            kb = key_block(kj)
            near = is_near(kj)
            c = jnp.where(near, 0.0, jnp.where(kj < qi, c_left, c_right))
            for j, q_ref in enumerate((q1_ref, q2_ref)):
                s_ref[...] = _dot(kb, q_ref[...])

                @pl.when(near)
                def _():
                    s_ref[...] = s_ref[...] + bias_ref[jnp.clip(kj - qi + 1, 0, NEAR_STEPS - 1)]

                s = s_ref[...]
                m_old = m_ref[j:j + 1, :]
                m_new = jnp.maximum(m_old, jnp.max(s, axis=0, keepdims=True) + c)
                p = jnp.exp2(s - (m_new - c))
                alpha = jnp.exp2(m_old - m_new)
                l8_ref[j] = alpha * l8_ref[j] + p.reshape(t // 8, 8, t).sum(axis=0)
                acc_ref[j] = alpha * acc_ref[j] + _dot(vt_ref[kj], p.astype(BF16))
                m_ref[j:j + 1, :] = m_new
            return carry

        lax.fori_loop(0, nk, body, 0)

    lv = lam_ref[...]
    lam = (jnp.exp(jnp.sum(lv[0:1] * lv[1:2], axis=1, keepdims=True))
           - jnp.exp(jnp.sum(lv[2:3] * lv[3:4], axis=1, keepdims=True)) + lambda_init)
    l1 = jnp.sum(l8_ref[0], axis=0, keepdims=True)
    l2 = jnp.sum(l8_ref[1], axis=0, keepdims=True)
    o = acc_ref[0] * (1.0 / l1) - lam * (acc_ref[1] * (1.0 / l2))
    on = o * lax.rsqrt(jnp.mean(o * o, axis=0, keepdims=True) + EPS) * g_ref[...]
    on = on * (1.0 - lambda_init)
    o_ref[...] = on.T.astype(BF16)


def _attn(qt, k, vt, bias, rel_bias, lam_vecs, subln_g, lambda_init):
    B, nb, n, t = qt.shape
    S = k.shape[1]
    assert nb % 2 == 0 and nb > NEAR_STEPS, nb
    assert t >= MAX_DIST
    dh2 = 2 * DH_B
    return pl.pallas_call(
        functools.partial(_attn_kernel, t=t, nk=nb, lambda_init=lambda_init),
        grid=(B, HB, nb),
        in_specs=[
            pl.BlockSpec((None, None, dh2, t), lambda b, h, i: (b, i, h, 0)),
            pl.BlockSpec((None, S, dh2), lambda b, h, i: (b, 0, h)),
            pl.BlockSpec((None, nb, dh2, t), lambda b, h, i: (b, 0, h, 0)),
            pl.BlockSpec((None, NEAR_STEPS, t, t), lambda b, h, i: (h, 0, 0, 0)),
            pl.BlockSpec(memory_space=pltpu.SMEM),
            _resident(lam_vecs.shape),
            _resident(subln_g.shape),
        ],
        out_specs=pl.BlockSpec((None, t, dh2), lambda b, h, i: (b, i, h)),
        out_shape=jax.ShapeDtypeStruct((B, S, n), BF16),
        scratch_shapes=[
            pltpu.VMEM((dh2, t), BF16),
            pltpu.VMEM((dh2, t), BF16),
            pltpu.VMEM((2, t), F32),
            pltpu.VMEM((2, dh2, t), F32),
            pltpu.VMEM((2, 8, t), F32),
            pltpu.VMEM((t, t), F32),
            pltpu.VMEM((2, t, t), BF16),
            pltpu.VMEM((2, t, t), BF16),
            pltpu.VMEM((2, 128), F32),
        ],
        compiler_params=_params("parallel", "parallel", "arbitrary"),
        name="diff_attn",
    )(qt, k, vt, bias, rel_bias, lam_vecs, subln_g)


def _mem_kv_kernel(mem_ref, g_ref, w_ref, kv_ref):
    mn = _rms(mem_ref[...], g_ref[...]).astype(BF16)
    kv_ref[...] = _dot(mn, w_ref[...]).astype(BF16)


def _mem_kv(mem, g, w_kv):
    B, M, D = mem.shape
    n = w_kv.shape[1]
    return pl.pallas_call(
        _mem_kv_kernel,
        grid=(B,),
        in_specs=[pl.BlockSpec((None, M, D), lambda b: (b, 0, 0)), _resident((1, D)),
                  _resident(w_kv.shape)],
        out_specs=pl.BlockSpec((None, M, n), lambda b: (b, 0, 0)),
        out_shape=jax.ShapeDtypeStruct((B, M, n), BF16),
        compiler_params=_params("parallel"),
        name="mem_kv",
    )(mem, g, w_kv)


def _tail_kernel(*refs, mixer, final_norm):
    if mixer == "mlstm":
        (hf_ref, hb_ref, o_ref, x_ref, ng_ref, wm_ref, gc_ref, wq_ref, kv_ref, wo_ref,
         gm_ref, w1_ref, w2_ref, gf_ref, out_ref) = refs
        parts = []
        for h in range(HA):
            sl = slice(h * DV_A, (h + 1) * DV_A)
            hh = hf_ref[:, sl] + hb_ref[:, sl]
            hn = hh * lax.rsqrt(jnp.mean(hh * hh, axis=-1, keepdims=True) + EPS)
            hn = hn * ng_ref[:, sl] * jax.nn.sigmoid(o_ref[:, sl])
            parts.append(hn.astype(BF16))
        mixed = jnp.concatenate(parts, axis=1)
    else:
        (o_ref, x_ref, wm_ref, gc_ref, wq_ref, kv_ref, wo_ref,
         gm_ref, w1_ref, w2_ref, gf_ref, out_ref) = refs
        mixed = o_ref[...]
    x = x_ref[...] + _dot(mixed, wm_ref[...])

    D = x.shape[-1]
    dc = D // HC
    xn = _rms(x, gc_ref[...]).astype(BF16)
    q = _dot(xn, wq_ref[...])
    parts = []
    for h in range(HC):
        qh = (q[:, h * dc:(h + 1) * dc] * (dc ** -0.5)).astype(BF16)
        kh = kv_ref[:, h * dc:(h + 1) * dc]
        vh = kv_ref[:, D + h * dc:D + (h + 1) * dc]
        s = lax.dot_general(qh, kh, NT_DIMS, preferred_element_type=F32)
        p = jnp.exp(s - jnp.max(s, axis=-1, keepdims=True))
        inv = 1.0 / jnp.sum(p, axis=-1, keepdims=True)
        parts.append((_dot(p.astype(BF16), vh) * inv).astype(BF16))
    x = x + _dot(jnp.concatenate(parts, axis=1), wo_ref[...])

    xn = _rms(x, gm_ref[...]).astype(BF16)
    acc = x
    for c in range(w1_ref.shape[1] // D):
        hcol = _dot(xn, w1_ref[:, c * D:(c + 1) * D])
        hcol = jnp.square(jnp.maximum(hcol, 0.0)).astype(BF16)
        acc = acc + _dot(hcol, w2_ref[c * D:(c + 1) * D, :])
    out_ref[...] = _rms(acc, gf_ref[...]) if final_norm else acc


def _tail(mixer, mixer_inputs, x, mixer_consts, g_cross, w_q, kv, w_out, g_mlp, w1, w2, g_final,
          final_norm):
    B, S, D = x.shape
    M = kv.shape[1]
    ts = min(TAIL_TILE, S)
    tile = lambda n: pl.BlockSpec((None, ts, n), lambda b, i: (b, i, 0))
    in_specs = ([tile(a.shape[-1]) for a in mixer_inputs] + [tile(D)]
                + [_resident(c.shape) for c in mixer_consts]
                + [_resident((1, D)), _resident(w_q.shape),
                   pl.BlockSpec((None, M, 2 * D), lambda b, i: (b, 0, 0)),
                   _resident(w_out.shape), _resident((1, D)), _resident(w1.shape),
                   _resident(w2.shape), _resident((1, D))])
    return pl.pallas_call(
        functools.partial(_tail_kernel, mixer=mixer, final_norm=final_norm),
        grid=(B, S // ts),
        in_specs=in_specs,
        out_specs=tile(D),
        out_shape=jax.ShapeDtypeStruct((B, S, D), F32),
        compiler_params=_params("parallel", "parallel"),
        name="tail_" + mixer,
    )(*mixer_inputs, x, *mixer_consts, g_cross, w_q, kv, w_out, g_mlp, w1, w2, g_final)


def _prepare(g_mix, g_cross, g_mem, g_mlp, g_final, a_w_in, a_w_gate, a_b_gate, a_norm_g,
             a_w_out, b_w_qkv, b_lambda, b_subln_g, b_w_out, rel_bias, c_w_q, c_w_kv,
             c_w_out, f_w1, f_w2):
    row = lambda v: v.reshape(1, -1).astype(F32)
    nq = 2 * HB * DH_B
    wqkv = b_w_qkv[0]
    return dict(
        g_mix=[row(g) for g in g_mix], g_cross=[row(g) for g in g_cross],
        g_mem=[row(g) for g in g_mem], g_mlp=[row(g) for g in g_mlp], g_final=row(g_final),
        a_w_in=a_w_in[0].astype(BF16), a_w_gate_t=a_w_gate[0].T.astype(BF16),
        a_b_gate=a_b_gate[0].reshape(-1, 1).astype(F32), a_norm_g=row(a_norm_g[0]),
        a_w_out=a_w_out[0].astype(BF16),
        b_w_q_t=wqkv[:, :nq].T.astype(BF16), b_w_k=wqkv[:, nq:2 * nq].astype(BF16),
        b_w_v_t=wqkv[:, 2 * nq:].T.astype(BF16),
        b_lambda=b_lambda[0].astype(F32), b_subln_g=b_subln_g[0].reshape(-1, 1).astype(F32),
        b_w_out=b_w_out[0].astype(BF16),
        rel_bias=rel_bias.astype(F32),
        c_w_q=[w.astype(BF16) for w in c_w_q], c_w_kv=[w.astype(BF16) for w in c_w_kv],
        c_w_out=[w.astype(BF16) for w in c_w_out],
        f_w1=[w.astype(BF16) for w in f_w1], f_w2=[w.astype(BF16) for w in f_w2],
    )


def _encoder(x, mem, p, bias, t):
    depth = len(p["g_mix"])
    for i in range(depth):
        if i % 2 == 0:
            qkv, o, gt = _proj_a(x, p["g_mix"][i], p["a_w_in"], p["a_w_gate_t"], p["a_b_gate"])
            hf, hb = _mlstm(qkv, gt)
            mixer, mixer_inputs, mixer_consts = "mlstm", (hf, hb, o), (p["a_norm_g"], p["a_w_out"])
        else:
            qt, k, vt = _proj_b(x, p["g_mix"][i], p["b_w_q_t"], p["b_w_k"], p["b_w_v_t"], t)
            o = _attn(qt, k, vt, bias, p["rel_bias"], p["b_lambda"], p["b_subln_g"], _lambda_init(i))
            mixer, mixer_inputs, mixer_consts = "attn", (o,), (p["b_w_out"],)
        kv = _mem_kv(mem, p["g_mem"][i], p["c_w_kv"][i])
        x = _tail(mixer, mixer_inputs, x, mixer_consts, p["g_cross"][i], p["c_w_q"][i], kv,
                  p["c_w_out"][i], p["g_mlp"][i], p["f_w1"][i], p["f_w2"][i], p["g_final"],
                  i == depth - 1)
    return x


def kernel(x_prompt, x_sample, mem_prompt, mem_sample, g_mix, g_cross, g_mem, g_mlp, g_final, a_w_in, a_w_gate, a_b_gate, a_norm_g, a_w_out, b_w_qkv, b_lambda, b_subln_g, b_w_out, rel_bias, c_w_q, c_w_kv, c_w_out, f_w1, f_w2):
    p = _prepare(g_mix, g_cross, g_mem, g_mlp, g_final, a_w_in, a_w_gate, a_b_gate, a_norm_g,
                 a_w_out, b_w_qkv, b_lambda, b_subln_g, b_w_out, rel_bias, c_w_q, c_w_kv,
                 c_w_out, f_w1, f_w2)
    t = min(TOKEN_TILE, x_prompt.shape[1], x_sample.shape[1])
    bias = _bias_tiles(p["rel_bias"], t)
    y_prompt = _encoder(x_prompt, mem_prompt, p, bias, t)
    y_sample = _encoder(x_sample, mem_sample, p, bias, t)
    return (y_prompt, y_sample)
```

```python
import functools
import math

import jax
import jax.numpy as jnp
from jax import lax
from jax.experimental import pallas as pl
from jax.experimental.pallas import tpu as pltpu

EPS = 1e-6
HA = 4
DK_A = 128
DV_A = 256
CHUNK = 128
MLSTM_ROWS = 2
MLSTM_GROUP = 16
HB = 8
DH_B = 64
N_BUCKETS = 32
MAX_DIST = 128
HC = 4

F32 = jnp.float32
BF16 = jnp.bfloat16

VMEM_LIMIT_BYTES = 56 * 1024 * 1024
TOKEN_TILE = 512
TAIL_TILE = 512
NT_DIMS = (((1,), (1,)), ((), ()))
LOG2E = math.log2(math.e)
NEAR_STEPS = 3
FAR_UNROLL_MAX = 14
NORM_MARGIN = 1.01
BOUND_SHIFT_MAX_SPREAD = 100.0


def _lambda_init(layer):
    return 0.8 - 0.6 * math.exp(-0.3 * layer)


def _params(*sem):
    return pltpu.CompilerParams(dimension_semantics=sem, vmem_limit_bytes=VMEM_LIMIT_BYTES)


def _resident(shape):
    zeros = (0,) * len(shape)
    return pl.BlockSpec(shape, lambda *_: zeros, pipeline_mode=pl.Buffered(1))


def _rms(x, g):
    return x * lax.rsqrt(jnp.mean(x * x, axis=-1, keepdims=True) + EPS) * g


def _dot(a, b):
    return jnp.dot(a, b, preferred_element_type=F32)


def _proj_a_kernel(x_ref, g_ref, w_ref, wgt_ref, bg_ref, qkv_ref, o_ref, gt_ref):
    xn = _rms(x_ref[...], g_ref[...]).astype(BF16)
    nqk = HA * DK_A
    for c in range(4):
        y = _dot(xn, w_ref[:, c * nqk:(c + 1) * nqk])
        if c == 1:
            y = y * (DK_A ** -0.5)
        qkv_ref[:, c * nqk:(c + 1) * nqk] = y.astype(BF16)
    for c in range(2):
        o_ref[:, c * nqk:(c + 1) * nqk] = _dot(xn, w_ref[:, (4 + c) * nqk:(5 + c) * nqk])
    gt = lax.dot_general(wgt_ref[...], xn, NT_DIMS, preferred_element_type=F32)
    gt_ref[...] = gt + bg_ref[...]


def _proj_a(x, g, w_in, w_gate_t, b_gate):
    B, S, D = x.shape
    ts = min(TOKEN_TILE, S)
    n_qkv = 2 * HA * DK_A + HA * DV_A
    n_o = HA * DV_A
    return pl.pallas_call(
        _proj_a_kernel,
        grid=(B, S // ts),
        in_specs=[
            pl.BlockSpec((None, ts, D), lambda b, i: (b, i, 0)),
            _resident((1, D)),
            _resident(w_in.shape),
            _resident(w_gate_t.shape),
            _resident(b_gate.shape),
        ],
        out_specs=[
            pl.BlockSpec((None, ts, n_qkv), lambda b, i: (b, i, 0)),
            pl.BlockSpec((None, ts, n_o), lambda b, i: (b, i, 0)),
            pl.BlockSpec((None, 4 * HA, ts), lambda b, i: (b, 0, i)),
        ],
        out_shape=[
            jax.ShapeDtypeStruct((B, S, n_qkv), BF16),
            jax.ShapeDtypeStruct((B, S, n_o), F32),
            jax.ShapeDtypeStruct((B, 4 * HA, S), F32),
        ],
        compiler_params=_params("parallel", "parallel"),
        name="proj_a",
    )(x, g, w_in, w_gate_t, b_gate)


def _log_sigmoid(x):
    return jnp.minimum(x, 0.0) - jnp.log1p(jnp.exp(-jnp.abs(x)))


def _mlstm_gate_prep(gf, gb):
    L = CHUNK
    row8 = lax.broadcasted_iota(jnp.int32, (2 * HA, L), 0)
    lane8 = lax.broadcasted_iota(jnp.int32, (2 * HA, L), 1)
    is_f = row8 < HA
    ig8 = jnp.where(is_f, gf, pltpu.roll(gb, HA, 0))
    fg8 = jnp.where(is_f, pltpu.roll(gf, HA, 0), gb)
    lf8 = _log_sigmoid(fg8)

    b8 = lf8
    d = 1
    while d < L:
        fwd = jnp.where(lane8 >= d, pltpu.roll(b8, d, 1), 0.0)
        bwd = jnp.where(lane8 < L - d, pltpu.roll(b8, L - d, 1), 0.0)
        b8 = b8 + jnp.where(is_f, fwd, bwd)
        d *= 2

    bl8 = jnp.where(is_f[:, 0:1], b8[:, L - 1:L], b8[:, 0:1])
    a8 = ig8 - b8
    wlog8 = bl8 + a8
    mw8 = jnp.max(wlog8, axis=1, keepdims=True)
    return b8, a8, jnp.exp(wlog8 - mw8), bl8, mw8


def _mlstm_kernel(qf_ref, kf_ref, vf_ref, qb_ref, kb_ref, vb_ref, gf_ref, gb_ref, gfn_ref, gbn_ref,
                  hf_ref, hb_ref, c_ref, m_ref, prep_ref, sc_ref):
    L = CHUNK
    nh2 = 2 * HA
    nrows = gf_ref.shape[0]

    def store_prep(g_f, g_b):
        for bi in range(nrows):
            b8, a8, w8, bl8, mw8 = _mlstm_gate_prep(g_f[bi], g_b[bi])
            prep_ref[bi, 0] = b8
            prep_ref[bi, 1] = a8
            prep_ref[bi, 2] = w8
            sc_ref[bi, 0:nh2, :] = jnp.broadcast_to(bl8, (nh2, sc_ref.shape[2]))
            sc_ref[bi, nh2:2 * nh2, :] = jnp.broadcast_to(mw8, (nh2, sc_ref.shape[2]))

    @pl.when(pl.program_id(1) == 0)
    def _():
        c_ref[...] = jnp.zeros(c_ref.shape, F32)
        m_ref[...] = jnp.zeros(m_ref.shape, F32)
        store_prep(gf_ref, gb_ref)

    ti = lax.broadcasted_iota(jnp.int32, (L, L), 0)
    si = lax.broadcasted_iota(jnp.int32, (L, L), 1)
    ones = jnp.ones((L, 128), BF16)

    def lanes(x, reps):
        return jnp.concatenate([x] * reps, axis=1)

    def refs_of(r):
        if r < HA:
            return qf_ref, kf_ref, vf_ref, hf_ref, si <= ti
        return qb_ref, kb_ref, vb_ref, hb_ref, si >= ti

    gates = []
    for bi in range(nrows):
        bl8 = sc_ref[bi, 0:nh2, 0:1]
        mw8 = sc_ref[bi, nh2:2 * nh2, 0:1]
        m_old8 = m_ref[bi, :, 0:1]
        m_new8 = jnp.maximum(bl8 + m_old8, mw8)
        gates.append(dict(b8=prep_ref[bi, 0], a8=prep_ref[bi, 1],
                          w8=prep_ref[bi, 2],
                          m_old8=m_old8, m_new8=m_new8,
                          decay8=jnp.exp(bl8 + m_old8 - m_new8), fw8=jnp.exp(mw8 - m_new8)))

    chains = [(bi, r) for r in range(nh2) for bi in range(nrows)]
    for g0 in range(0, len(chains), MLSTM_GROUP):
        group = chains[g0:g0 + MLSTM_GROUP]
        st = {}
        for bi, r in group:
            q_ref, k_ref, v_ref, out_ref, mask = refs_of(r)
            g = gates[bi]
            h = r % HA
            q = q_ref[bi, :, h * DK_A:(h + 1) * DK_A]
            k = k_ref[bi, :, h * DK_A:(h + 1) * DK_A]
            v1 = jnp.concatenate([v_ref[bi, :, h * DV_A:(h + 1) * DV_A], ones], axis=1)
            b_mat = jnp.broadcast_to(g["b8"][r:r + 1, :], (L, L)).T
            w_mat = jnp.broadcast_to(g["w8"][r:r + 1, :], (L, L)).T
            dlog = jnp.where(mask, b_mat + g["a8"][r:r + 1, :], -jnp.inf)
            st[bi, r] = dict(q=q, k=k, v1=v1, w_mat=w_mat, dlog=dlog,
                             inter=b_mat + g["m_old8"][r:r + 1, :],
                             qk=lax.dot_general(q, k, NT_DIMS, preferred_element_type=F32))
        for key in group:
            d = st[key]
            rmax = jnp.broadcast_to(jnp.max(d["dlog"], axis=1, keepdims=True), (L, L))
            d["m_t"] = jnp.maximum(d["inter"], rmax)
        for bi, r in group:
            d = st[bi, r]
            d["s"] = d["qk"] * jnp.exp(d["dlog"] - d["m_t"])
            d["iw"] = jnp.exp(d["inter"] - d["m_t"])
            d["c_old"] = c_ref[bi, r]
        for bi, r in group:
            d = st[bi, r]
            _, _, _, out_ref, _ = refs_of(r)
            h = r % HA
            numden = (lanes(d["iw"], 3) * _dot(d["q"], d["c_old"].astype(BF16))
                      + _dot(d["s"].astype(BF16), d["v1"]))
            den = numden[:, DV_A:]
            inv = 1.0 / jnp.maximum(jnp.abs(den), jnp.exp(-d["m_t"]))
            out_ref[bi, :, h * DV_A:(h + 1) * DV_A] = numden[:, :DV_A] * lanes(inv, 2)
        for bi, r in group:
            d = st[bi, r]
            g = gates[bi]
            wk = d["w_mat"] * d["k"].astype(F32)
            c_ref[bi, r] = (g["decay8"][r:r + 1, :] * d["c_old"]
                            + g["fw8"][r:r + 1, :] * _dot(wk.T.astype(BF16), d["v1"]))

    for bi in range(nrows):
        m_ref[bi] = jnp.broadcast_to(gates[bi]["m_new8"], m_ref.shape[1:])
    store_prep(gfn_ref, gbn_ref)


def _mlstm(qkv, gt):
    B, S, _ = qkv.shape
    L = CHUNK
    nc = S // L
    nqk = HA * DK_A
    nv = HA * DV_A

    def fwd(col):
        return lambda b, c: (b, c, col)

    def bwd(col):
        return lambda b, c: (b, nc - 1 - c, col)

    nr = MLSTM_ROWS
    assert B % nr == 0, B
    gates = lambda index_map: pl.BlockSpec((nr, 2 * HA, L), index_map)
    return pl.pallas_call(
        _mlstm_kernel,
        grid=(B // nr, nc),
        in_specs=[
            pl.BlockSpec((nr, L, nqk), fwd(0)),
            pl.BlockSpec((nr, L, nqk), fwd(1)),
            pl.BlockSpec((nr, L, nv), fwd(1)),
            pl.BlockSpec((nr, L, nqk), bwd(0)),
            pl.BlockSpec((nr, L, nqk), bwd(1)),
            pl.BlockSpec((nr, L, nv), bwd(1)),
            gates(lambda b, c: (b, 0, c)),
            gates(lambda b, c: (b, 1, nc - 1 - c)),
            gates(lambda b, c: (b, 0, jnp.minimum(c + 1, nc - 1))),
            gates(lambda b, c: (b, 1, jnp.maximum(nc - 2 - c, 0))),
        ],
        out_specs=[
            pl.BlockSpec((nr, L, nv), fwd(0)),
            pl.BlockSpec((nr, L, nv), bwd(0)),
        ],
        out_shape=[jax.ShapeDtypeStruct((B, S, nv), F32)] * 2,
        scratch_shapes=[
            pltpu.VMEM((nr, 2 * HA, DK_A, DV_A + 128), F32),
            pltpu.VMEM((nr, 2 * HA, 128), F32),
            pltpu.VMEM((nr, 3, 2 * HA, L), F32),
            pltpu.VMEM((nr, 4 * HA, 128), F32),
        ],
        compiler_params=_params("parallel", "arbitrary"),
        name="mlstm",
    )(qkv, qkv, qkv, qkv, qkv, qkv, gt, gt, gt, gt)


def _proj_b_kernel(x_ref, g_ref, wqt_ref, wk_ref, wvt_ref, qt_ref, k_ref, vt_ref):
    xn = _rms(x_ref[...], g_ref[...]).astype(BF16)
    qt = lax.dot_general(wqt_ref[...], xn, NT_DIMS, preferred_element_type=F32)
    qt_ref[...] = (qt * (DH_B ** -0.5 * LOG2E)).astype(BF16)
    vt = lax.dot_general(wvt_ref[...], xn, NT_DIMS, preferred_element_type=F32)
    vt_ref[...] = vt.astype(BF16)
    k_ref[...] = _dot(xn, wk_ref[...]).astype(BF16)


def _proj_b(x, g, w_q_t, w_k, w_v_t, t):
    B, S, D = x.shape
    nb = S // t
    n = w_k.shape[1]
    return pl.pallas_call(
        _proj_b_kernel,
        grid=(B, nb),
        in_specs=[
            pl.BlockSpec((None, t, D), lambda b, i: (b, i, 0)),
            _resident((1, D)),
            _resident(w_q_t.shape),
            _resident(w_k.shape),
            _resident(w_v_t.shape),
        ],
        out_specs=[
            pl.BlockSpec((None, None, n, t), lambda b, i: (b, i, 0, 0)),
            pl.BlockSpec((None, t, n), lambda b, i: (b, i, 0)),
            pl.BlockSpec((None, None, n, t), lambda b, i: (b, i, 0, 0)),
        ],
        out_shape=[
            jax.ShapeDtypeStruct((B, nb, n, t), BF16),
            jax.ShapeDtypeStruct((B, S, n), BF16),
            jax.ShapeDtypeStruct((B, nb, n, t), BF16),
        ],
        compiler_params=_params("parallel", "parallel"),
        name="proj_b",
    )(x, g, w_q_t, w_k, w_v_t)


def _bias_kernel(tab_ref, out_ref, *, t):
    h = pl.program_id(0)
    off = (pl.program_id(1) - 1) * t
    ki = lax.broadcasted_iota(jnp.int32, (t, t), 0)
    qi = lax.broadcasted_iota(jnp.int32, (t, t), 1)
    rp = off + ki - qi
    half = N_BUCKETS // 2
    max_exact = half // 2
    ret = (rp > 0).astype(jnp.int32) * half
    n = jnp.abs(rp)
    nf = jnp.maximum(n, 1).astype(F32)
    large = max_exact + (jnp.log(nf / max_exact) / math.log(MAX_DIST / max_exact)
                         * (half - max_exact)).astype(jnp.int32)
    large = jnp.minimum(large, half - 1)
    bucket = ret + jnp.where(n < max_exact, n, large)
    acc = jnp.zeros((t, t), F32)
    for b in range(N_BUCKETS):
        acc = jnp.where(bucket == b, tab_ref[b, h], acc)
    out_ref[...] = acc * LOG2E


def _bias_tiles(rel_bias, t):
    return pl.pallas_call(
        functools.partial(_bias_kernel, t=t),
        grid=(HB, 3),
        in_specs=[pl.BlockSpec(memory_space=pltpu.SMEM)],
        out_specs=pl.BlockSpec((None, None, t, t), lambda h, j: (h, j, 0, 0)),
        out_shape=jax.ShapeDtypeStruct((HB, 3, t, t), F32),
        compiler_params=_params("parallel", "parallel"),
        name="bias_tiles",
    )(rel_bias)


def _attn_kernel(qt_ref, k_ref, vt_ref, bias_ref, tab_ref, lam_ref, g_ref, o_ref,
                 q1_ref, q2_ref, m_ref, acc_ref, l8_ref, s_ref, p0_ref, p1_ref, kn_ref,
                 *, t, nk, lambda_init):
    h = pl.program_id(1)
    qi = pl.program_id(2)
    p_refs = (p0_ref, p1_ref)
    dv = 2 * DH_B

    @pl.when(qi == 0)
    def _():
        first_half = lax.broadcasted_iota(jnp.int32, (t, dv), 1) < DH_B

        def body(i, carry):
            kf = k_ref[pl.ds(pl.multiple_of(i * t, t), t), :].astype(F32)
            sq = kf * kf
            n1 = jnp.sum(jnp.where(first_half, sq, 0.0), axis=1, keepdims=True)
            n2 = jnp.sum(jnp.where(first_half, 0.0, sq), axis=1, keepdims=True)
            return (jnp.maximum(carry[0], jnp.max(n1, axis=0, keepdims=True)),
                    jnp.maximum(carry[1], jnp.max(n2, axis=0, keepdims=True)))

        zero = jnp.zeros((1, 1), F32)
        n1, n2 = lax.fori_loop(0, nk, body, (zero, zero))
        kn_ref[0:1, :] = jnp.broadcast_to(n1, (1, kn_ref.shape[1]))
        kn_ref[1:2, :] = jnp.broadcast_to(n2, (1, kn_ref.shape[1]))

    qt = qt_ref[...]
    rows = lax.broadcasted_iota(jnp.int32, qt.shape, 0)
    q1_ref[...] = jnp.where(rows < DH_B, qt, jnp.zeros_like(qt))
    q2_ref[...] = jnp.where(rows >= DH_B, qt, jnp.zeros_like(qt))
    acc_ref[...] = jnp.zeros(acc_ref.shape, F32)
    l8_ref[...] = jnp.zeros(l8_ref.shape, F32)

    c_left = tab_ref[N_BUCKETS // 2 - 1, h] * LOG2E
    c_right = tab_ref[N_BUCKETS - 1, h] * LOG2E
    b_max = tab_ref[0, h]
    b_min = tab_ref[0, h]
    for b in range(1, N_BUCKETS):
        b_max = jnp.maximum(b_max, tab_ref[b, h])
        b_min = jnp.minimum(b_min, tab_ref[b, h])
    b_max = b_max * LOG2E
    b_min = b_min * LOG2E

    def is_near(kj):
        return jnp.abs(kj - qi) <= 1

    def block_of(n):
        x = qi - 1 + n
        return jnp.where(x < 0, x + nk, jnp.where(x >= nk, x - nk, x))

    def key_block(kj):
        return k_ref[pl.ds(pl.multiple_of(kj * t, t), t), :]

    spread = jnp.zeros((1, t), F32)
    for j, q_ref in enumerate((q1_ref, q2_ref)):
        qf = q_ref[...].astype(F32)
        qn2 = jnp.sum(qf * qf, axis=0, keepdims=True)
        bound = jnp.sqrt(qn2 * kn_ref[j:j + 1, 0:1]) * NORM_MARGIN + 1.0
        m_ref[j:j + 1, :] = bound + b_max
        spread = jnp.maximum(spread, 2.0 * bound + (b_max - b_min))
    safe = jnp.max(spread) <= BOUND_SHIFT_MAX_SPREAD

    def bound_ab(n, par, maybe_near):
        kj = block_of(n)
        kb = key_block(kj)
        c = jnp.where(kj < qi, c_left, c_right)
        if maybe_near:
            near = is_near(kj)
            c = jnp.where(near, 0.0, c)
            w_bias = jnp.where(near, 1.0, 0.0)
            bt = bias_ref[jnp.clip(kj - qi + 1, 0, NEAR_STEPS - 1)] * w_bias
        for j, q_ref in enumerate((q1_ref, q2_ref)):
            s = _dot(kb, q_ref[...])
            if maybe_near:
                s = s + bt
            p = jnp.exp2(s - (m_ref[j:j + 1, :] - c))
            l8_ref[j] = l8_ref[j] + p.reshape(t // 8, 8, t).sum(axis=0)
            p_refs[par][j] = p.astype(BF16)

    def bound_c(n, par):
        vt = vt_ref[block_of(n)]
        for j in range(2):
            acc_ref[j] = acc_ref[j] + _dot(vt, p_refs[par][j])

    @pl.when(safe)
    def _():
        def steps(n0, count, maybe_near):
            for u in range(count):
                par = (n0 + u) % 2 if isinstance(n0, int) else (NEAR_STEPS + u) % 2
                bound_ab(n0 + u + 1, 1 - par, maybe_near)
                bound_c(n0 + u, par)

        n_far = nk - 1 - NEAR_STEPS
        unroll = max([u for u in range(2, FAR_UNROLL_MAX + 1, 2)
                      if n_far % u == 0 and n_far // u >= 2] or [2])

        def body(i, carry):
            steps(NEAR_STEPS + unroll * i, unroll, False)
            return carry

        bound_ab(0, 0, True)
        steps(0, NEAR_STEPS - 1, True)
        steps(NEAR_STEPS - 1, 1, False)
        lax.fori_loop(0, n_far // unroll, body, 0)
        bound_c(nk - 1, (nk - 1) % 2)

    @pl.when(jnp.logical_not(safe))
    def _():
        m_ref[...] = jnp.full(m_ref.shape, -jnp.inf, F32)

        def body(n, carry):
            kj = block_of(n)
            kb = key_block(kj)
            near = is_near(kj)
            c = jnp.where(near, 0.0, jnp.where(kj < qi, c_left, c_right))
            for j, q_ref in enumerate((q1_ref, q2_ref)):
                s_ref[...] = _dot(kb, q_ref[...])

                @pl.when(near)
                def _():
                    s_ref[...] = s_ref[...] + bias_ref[jnp.clip(kj - qi + 1, 0, NEAR_STEPS - 1)]

                s = s_ref[...]
                m_old = m_ref[j:j + 1, :]
                m_new = jnp.maximum(m_old, jnp.max(s, axis=0, keepdims=True) + c)
                p = jnp.exp2(s - (m_new - c))
                alpha = jnp.exp2(m_old - m_new)
                l8_ref[j] = alpha * l8_ref[j] + p.reshape(t // 8, 8, t).sum(axis=0)
                acc_ref[j] = alpha * acc_ref[j] + _dot(vt_ref[kj], p.astype(BF16))
                m_ref[j:j + 1, :] = m_new
            return carry

        lax.fori_loop(0, nk, body, 0)

    lv = lam_ref[...]
    lam = (jnp.exp(jnp.sum(lv[0:1] * lv[1:2], axis=1, keepdims=True))
           - jnp.exp(jnp.sum(lv[2:3] * lv[3:4], axis=1, keepdims=True)) + lambda_init)
    l1 = jnp.sum(l8_ref[0], axis=0, keepdims=True)
    l2 = jnp.sum(l8_ref[1], axis=0, keepdims=True)
    o = acc_ref[0] * (1.0 / l1) - lam * (acc_ref[1] * (1.0 / l2))
    on = o * lax.rsqrt(jnp.mean(o * o, axis=0, keepdims=True) + EPS) * g_ref[...]
    on = on * (1.0 - lambda_init)
    o_ref[...] = on.T.astype(BF16)


def _attn(qt, k, vt, bias, rel_bias, lam_vecs, subln_g, lambda_init):
    B, nb, n, t = qt.shape
    S = k.shape[1]
    assert nb % 2 == 0 and nb > NEAR_STEPS, nb
    assert t >= MAX_DIST
    dh2 = 2 * DH_B
    return pl.pallas_call(
        functools.partial(_attn_kernel, t=t, nk=nb, lambda_init=lambda_init),
        grid=(B, HB, nb),
        in_specs=[
            pl.BlockSpec((None, None, dh2, t), lambda b, h, i: (b, i, h, 0)),
            pl.BlockSpec((None, S, dh2), lambda b, h, i: (b, 0, h)),
            pl.BlockSpec((None, nb, dh2, t), lambda b, h, i: (b, 0, h, 0)),
            pl.BlockSpec((None, NEAR_STEPS, t, t), lambda b, h, i: (h, 0, 0, 0)),
            pl.BlockSpec(memory_space=pltpu.SMEM),
            _resident(lam_vecs.shape),
            _resident(subln_g.shape),
        ],
        out_specs=pl.BlockSpec((None, t, dh2), lambda b, h, i: (b, i, h)),
        out_shape=jax.ShapeDtypeStruct((B, S, n), BF16),
        scratch_shapes=[
            pltpu.VMEM((dh2, t), BF16),
            pltpu.VMEM((dh2, t), BF16),
            pltpu.VMEM((2, t), F32),
            pltpu.VMEM((2, dh2, t), F32),
            pltpu.VMEM((2, 8, t), F32),
            pltpu.VMEM((t, t), F32),
            pltpu.VMEM((2, t, t), BF16),
            pltpu.VMEM((2, t, t), BF16),
            pltpu.VMEM((2, 128), F32),
        ],
        compiler_params=_params("parallel", "parallel", "arbitrary"),
        name="diff_attn",
    )(qt, k, vt, bias, rel_bias, lam_vecs, subln_g)


def _mem_kv_kernel(mem_ref, g_ref, w_ref, kv_ref):
    mn = _rms(mem_ref[...], g_ref[...]).astype(BF16)
    kv_ref[...] = _dot(mn, w_ref[...]).astype(BF16)


def _mem_kv(mem, g, w_kv):
    B, M, D = mem.shape
    n = w_kv.shape[1]
    return pl.pallas_call(
        _mem_kv_kernel,
        grid=(B,),
        in_specs=[pl.BlockSpec((None, M, D), lambda b: (b, 0, 0)), _resident((1, D)),
                  _resident(w_kv.shape)],
        out_specs=pl.BlockSpec((None, M, n), lambda b: (b, 0, 0)),
        out_shape=jax.ShapeDtypeStruct((B, M, n), BF16),
        compiler_params=_params("parallel"),
        name="mem_kv",
    )(mem, g, w_kv)


def _tail_kernel(*refs, mixer, final_norm):
    if mixer == "mlstm":
        (hf_ref, hb_ref, o_ref, x_ref, ng_ref, wm_ref, gc_ref, wq_ref, kv_ref, wo_ref,
         gm_ref, w1_ref, w2_ref, gf_ref, out_ref) = refs
        parts = []
        for h in range(HA):
            sl = slice(h * DV_A, (h + 1) * DV_A)
            hh = hf_ref[:, sl] + hb_ref[:, sl]
            hn = hh * lax.rsqrt(jnp.mean(hh * hh, axis=-1, keepdims=True) + EPS)
            hn = hn * ng_ref[:, sl] * jax.nn.sigmoid(o_ref[:, sl])
            parts.append(hn.astype(BF16))
        mixed = jnp.concatenate(parts, axis=1)
    else:
        (o_ref, x_ref, wm_ref, gc_ref, wq_ref, kv_ref, wo_ref,
         gm_ref, w1_ref, w2_ref, gf_ref, out_ref) = refs
        mixed = o_ref[...]
    x = x_ref[...] + _dot(mixed, wm_ref[...])

    D = x.shape[-1]
    dc = D // HC
    xn = _rms(x, gc_ref[...]).astype(BF16)
    q = _dot(xn, wq_ref[...])
    parts = []
    for h in range(HC):
        qh = (q[:, h * dc:(h + 1) * dc] * (dc ** -0.5)).astype(BF16)
        kh = kv_ref[:, h * dc:(h + 1) * dc]
        vh = kv_ref[:, D + h * dc:D + (h + 1) * dc]
        s = lax.dot_general(qh, kh, NT_DIMS, preferred_element_type=F32)
        p = jnp.exp(s - jnp.max(s, axis=-1, keepdims=True))
        inv = 1.0 / jnp.sum(p, axis=-1, keepdims=True)
        parts.append((_dot(p.astype(BF16), vh) * inv).astype(BF16))
    x = x + _dot(jnp.concatenate(parts, axis=1), wo_ref[...])

    xn = _rms(x, gm_ref[...]).astype(BF16)
    acc = x
    for c in range(w1_ref.shape[1] // D):
        hcol = _dot(xn, w1_ref[:, c * D:(c + 1) * D])
        hcol = jnp.square(jnp.maximum(hcol, 0.0)).astype(BF16)
        acc = acc + _dot(hcol, w2_ref[c * D:(c + 1) * D, :])
    out_ref[...] = _rms(acc, gf_ref[...]) if final_norm else acc


def _tail(mixer, mixer_inputs, x, mixer_consts, g_cross, w_q, kv, w_out, g_mlp, w1, w2, g_final,
          final_norm):
    B, S, D = x.shape
    M = kv.shape[1]
    ts = min(TAIL_TILE, S)
    tile = lambda n: pl.BlockSpec((None, ts, n), lambda b, i: (b, i, 0))
    in_specs = ([tile(a.shape[-1]) for a in mixer_inputs] + [tile(D)]
                + [_resident(c.shape) for c in mixer_consts]
                + [_resident((1, D)), _resident(w_q.shape),
                   pl.BlockSpec((None, M, 2 * D), lambda b, i: (b, 0, 0)),
                   _resident(w_out.shape), _resident((1, D)), _resident(w1.shape),
                   _resident(w2.shape), _resident((1, D))])
    return pl.pallas_call(
        functools.partial(_tail_kernel, mixer=mixer, final_norm=final_norm),
        grid=(B, S // ts),
        in_specs=in_specs,
        out_specs=tile(D),
        out_shape=jax.ShapeDtypeStruct((B, S, D), F32),
        compiler_params=_params("parallel", "parallel"),
        name="tail_" + mixer,
    )(*mixer_inputs, x, *mixer_consts, g_cross, w_q, kv, w_out, g_mlp, w1, w2, g_final)


def _prepare(g_mix, g_cross, g_mem, g_mlp, g_final, a_w_in, a_w_gate, a_b_gate, a_norm_g,
             a_w_out, b_w_qkv, b_lambda, b_subln_g, b_w_out, rel_bias, c_w_q, c_w_kv,
             c_w_out, f_w1, f_w2):
    row = lambda v: v.reshape(1, -1).astype(F32)
    nq = 2 * HB * DH_B
    wqkv = b_w_qkv[0]
    return dict(
        g_mix=[row(g) for g in g_mix], g_cross=[row(g) for g in g_cross],
        g_mem=[row(g) for g in g_mem], g_mlp=[row(g) for g in g_mlp], g_final=row(g_final),
        a_w_in=a_w_in[0].astype(BF16), a_w_gate_t=a_w_gate[0].T.astype(BF16),
        a_b_gate=a_b_gate[0].reshape(-1, 1).astype(F32), a_norm_g=row(a_norm_g[0]),
        a_w_out=a_w_out[0].astype(BF16),
        b_w_q_t=wqkv[:, :nq].T.astype(BF16), b_w_k=wqkv[:, nq:2 * nq].astype(BF16),
        b_w_v_t=wqkv[:, 2 * nq:].T.astype(BF16),
        b_lambda=b_lambda[0].astype(F32), b_subln_g=b_subln_g[0].reshape(-1, 1).astype(F32),
        b_w_out=b_w_out[0].astype(BF16),
        rel_bias=rel_bias.astype(F32),
        c_w_q=[w.astype(BF16) for w in c_w_q], c_w_kv=[w.astype(BF16) for w in c_w_kv],
        c_w_out=[w.astype(BF16) for w in c_w_out],
        f_w1=[w.astype(BF16) for w in f_w1], f_w2=[w.astype(BF16) for w in f_w2],
    )


def _encoder(x, mem, p, bias, t):
    depth = len(p["g_mix"])
    for i in range(depth):
        if i % 2 == 0:
            qkv, o, gt = _proj_a(x, p["g_mix"][i], p["a_w_in"], p["a_w_gate_t"], p["a_b_gate"])
            hf, hb = _mlstm(qkv, gt)
            mixer, mixer_inputs, mixer_consts = "mlstm", (hf, hb, o), (p["a_norm_g"], p["a_w_out"])
        else:
            qt, k, vt = _proj_b(x, p["g_mix"][i], p["b_w_q_t"], p["b_w_k"], p["b_w_v_t"], t)
            o = _attn(qt, k, vt, bias, p["rel_bias"], p["b_lambda"], p["b_subln_g"], _lambda_init(i))
            mixer, mixer_inputs, mixer_consts = "attn", (o,), (p["b_w_out"],)
        kv = _mem_kv(mem, p["g_mem"][i], p["c_w_kv"][i])
        x = _tail(mixer, mixer_inputs, x, mixer_consts, p["g_cross"][i], p["c_w_q"][i], kv,
                  p["c_w_out"][i], p["g_mlp"][i], p["f_w1"][i], p["f_w2"][i], p["g_final"],
                  i == depth - 1)
    return x


def kernel(x_prompt, x_sample, mem_prompt, mem_sample, g_mix, g_cross, g_mem, g_mlp, g_final, a_w_in, a_w_gate, a_b_gate, a_norm_g, a_w_out, b_w_qkv, b_lambda, b_subln_g, b_w_out, rel_bias, c_w_q, c_w_kv, c_w_out, f_w1, f_w2):
    p = _prepare(g_mix, g_cross, g_mem, g_mlp, g_final, a_w_in, a_w_gate, a_b_gate, a_norm_g,
                 a_w_out, b_w_qkv, b_lambda, b_subln_g, b_w_out, rel_bias, c_w_q, c_w_kv,
                 c_w_out, f_w1, f_w2)
    t = min(TOKEN_TILE, x_prompt.shape[1], x_sample.shape[1])
    bias = _bias_tiles(p["rel_bias"], t)
    y_prompt = _encoder(x_prompt, mem_prompt, p, bias, t)
    y_sample = _encoder(x_sample, mem_sample, p, bias, t)
    return (y_prompt, y_sample)
```
